```python
import functools
import jax, jax.numpy as jnp
from jax import lax
import numpy as np

D_MODEL = 2048
BATCH = 8
SEQ = 2048
DEPTH = 4

D_FF = 5632
NORM_EPS = 1e-5
BLOCK = 128
NEG_INF = -1e30
MIXER_KINDS = ("gmlp", "swa", "fox")
N_MIXERS = 3
GMLP_WIDTH = D_MODEL
GMLP_CHUNK = 128
GMLP_GROUPS = 16
GMLP_GROUP_WIDTH = GMLP_WIDTH // GMLP_GROUPS
SWA_HEAD_DIM = 64
SWA_Q_HEADS = D_MODEL // SWA_HEAD_DIM
SWA_KV_HEADS = SWA_Q_HEADS // 8
SWA_WIDTH = SWA_Q_HEADS * SWA_HEAD_DIM
SWA_KV_WIDTH = SWA_KV_HEADS * SWA_HEAD_DIM
SWA_WINDOW = 128
ROPE_THETA = 500000.0
ROPE_DIM = SWA_HEAD_DIM // 4
FOX_HEAD_DIM = 128
FOX_HEADS = D_MODEL // FOX_HEAD_DIM
FOX_WIDTH = FOX_HEADS * FOX_HEAD_DIM

kernel_name = "hybrid_gmlp_swa_fox_macaron"


def rms_norm(x, g):
    x32 = x.astype(jnp.float32)
    y = x32 * lax.rsqrt(jnp.mean(x32 * x32, axis=-1, keepdims=True) + NORM_EPS)
    return (y * g.astype(jnp.float32)).astype(x.dtype)


def swiglu_ffn(h, wi, wo):
    gate, up = jnp.split(h @ wi, 2, axis=-1)
    return (jax.nn.silu(gate) * up) @ wo


def partial_rotary(x, pos):
    half = ROPE_DIM // 2
    inv_freq = ROPE_THETA ** (-(jnp.arange(half, dtype=jnp.float32) * 2.0 / ROPE_DIM))
    ang = pos.astype(jnp.float32)[:, None] * inv_freq[None, :]
    cos = jnp.cos(ang)[None, :, None, :]
    sin = jnp.sin(ang)[None, :, None, :]
    xr = x[..., :ROPE_DIM].astype(jnp.float32)
    x1, x2 = xr[..., :half], xr[..., half:]
    rot = jnp.concatenate([x1 * cos - x2 * sin, x2 * cos + x1 * sin], axis=-1).astype(x.dtype)
    return jnp.concatenate([rot, x[..., ROPE_DIM:]], axis=-1)


def gmlp_mixer(h, w_in, v_gain, w_s, b_s, w_out):
    B, S, _ = h.shape
    z = jax.nn.gelu(h @ w_in, approximate=False)
    u, v = jnp.split(z, 2, axis=-1)
    v = rms_norm(v, v_gain)
    nc = S // GMLP_CHUNK
    v = v.reshape(B, nc, GMLP_CHUNK, GMLP_GROUPS, GMLP_GROUP_WIDTH)
    causal = jnp.tril(jnp.ones((GMLP_CHUNK, GMLP_CHUNK), dtype=bool))
    ws = jnp.where(causal[None], w_s, jnp.zeros_like(w_s))
    mixed = jnp.einsum('gts,bcsgw->bctgw', ws, v) + b_s.T[:, :, None]
    y = u * mixed.reshape(B, S, GMLP_WIDTH)
    return y @ w_out


def swa_mixer(h, w_in, sinks, w_out):
    B, S, _ = h.shape
    nb = S // BLOCK
    G = SWA_Q_HEADS // SWA_KV_HEADS
    q, k, v = jnp.split(h @ w_in, [SWA_WIDTH, SWA_WIDTH + SWA_KV_WIDTH], axis=-1)
    q = q.reshape(B, S, SWA_Q_HEADS, SWA_HEAD_DIM)
    k = k.reshape(B, S, SWA_KV_HEADS, SWA_HEAD_DIM)
    v = v.reshape(B, S, SWA_KV_HEADS, SWA_HEAD_DIM)
    pos = jnp.arange(S)
    q = partial_rotary(q, pos)
    k = partial_rotary(k, pos)
    qb = q.reshape(B, nb, BLOCK, SWA_KV_HEADS, G, SWA_HEAD_DIM).astype(jnp.float32)

    def band(t):
        tb = t.reshape(B, nb, BLOCK, SWA_KV_HEADS, SWA_HEAD_DIM)
        prev = jnp.concatenate([jnp.zeros_like(tb[:, :1]), tb[:, :-1]], axis=1)
        return jnp.concatenate([prev, tb], axis=2).astype(jnp.float32)

    kb, vb = band(k), band(v)
    s = jnp.einsum('bnqkgd,bnskd->bnkgqs', qb, kb) * (SWA_HEAD_DIM ** -0.5)
    start = jnp.arange(nb)[:, None, None] * BLOCK
    qpos = start + jnp.arange(BLOCK)[None, :, None]
    kpos = start - BLOCK + jnp.arange(2 * BLOCK)[None, None, :]
    valid = (kpos <= qpos) & (qpos - kpos < SWA_WINDOW) & (kpos >= 0)
    s = jnp.where(valid[None, :, None, None], s, NEG_INF)
    sink = sinks.astype(jnp.float32).reshape(1, 1, SWA_KV_HEADS, G, 1, 1)
    m = jnp.maximum(jnp.max(s, axis=-1, keepdims=True), sink)
    p = jnp.exp(s - m)
    denom = jnp.sum(p, axis=-1, keepdims=True) + jnp.exp(sink - m)
    o = jnp.einsum('bnkgqs,bnskd->bnqkgd', p / denom, vb)
    o = o.reshape(B, S, SWA_WIDTH).astype(h.dtype)
    return o @ w_out


def fox_mixer(h, w_in, b_f, w_out):
    B, S, _ = h.shape
    nb = S // BLOCK
    q, k, v, fl = jnp.split(h @ w_in, [FOX_WIDTH, 2 * FOX_WIDTH, 3 * FOX_WIDTH], axis=-1)
    q = q.reshape(B, S, FOX_HEADS, FOX_HEAD_DIM)
    kf = k.reshape(B, S, FOX_HEADS, FOX_HEAD_DIM).astype(jnp.float32)
    vf = v.reshape(B, S, FOX_HEADS, FOX_HEAD_DIM).astype(jnp.float32)
    log_f = jax.nn.log_sigmoid(fl.astype(jnp.float32) + b_f.astype(jnp.float32))
    dec = jnp.cumsum(log_f, axis=1).transpose(0, 2, 1)
    kpos = jnp.arange(S)
    scale = FOX_HEAD_DIM ** -0.5

    def attend_block(i):
        start = i * BLOCK
        qb = lax.dynamic_slice_in_dim(q, start, BLOCK, axis=1).astype(jnp.float32)
        db = lax.dynamic_slice_in_dim(dec, start, BLOCK, axis=2)
        s = jnp.einsum('bqhd,bshd->bhqs', qb, kf) * scale + db[..., None] - dec[:, :, None, :]
        qpos = start + jnp.arange(BLOCK)
        s = jnp.where(kpos[None, :] <= qpos[:, None], s, NEG_INF)
        p = jax.nn.softmax(s, axis=-1)
        return jnp.einsum('bhqs,bshd->bqhd', p, vf)

    o = lax.map(attend_block, jnp.arange(nb))
    o = o.transpose(1, 0, 2, 3, 4).reshape(B, S, FOX_WIDTH).astype(h.dtype)
    return o @ w_out


def setup_inputs(seed: int = 0) -> dict:
    key = jax.random.key(seed)
    keys = iter(jax.random.split(key, 16 * DEPTH + 4))

    def nrm(shape, scale):
        return jax.random.normal(next(keys), shape, jnp.float32) * scale

    def gain(n):
        return 1.0 + 0.1 * nrm((n,), 1.0)

    inp = {"x": nrm((BATCH, SEQ, D_MODEL), 1.0)}
    for i in range(DEPTH):
        kind = MIXER_KINDS[i % N_MIXERS]
        p = f"l{i}_"
        inp[p + "ffn1_norm"] = gain(D_MODEL)
        inp[p + "ffn1_wi"] = nrm((D_MODEL, 2 * D_FF), D_MODEL ** -0.5)
        inp[p + "ffn1_wo"] = nrm((D_FF, D_MODEL), D_FF ** -0.5)
        inp[p + "mix_norm"] = gain(D_MODEL)
        if kind == "gmlp":
            inp[p + "mix_win"] = nrm((D_MODEL, 2 * GMLP_WIDTH), D_MODEL ** -0.5)
            inp[p + "gmlp_vnorm"] = gain(GMLP_WIDTH)
            inp[p + "gmlp_ws"] = nrm((GMLP_GROUPS, GMLP_CHUNK, GMLP_CHUNK), GMLP_CHUNK ** -0.5)
            inp[p + "gmlp_bs"] = 1.0 + 0.1 * nrm((GMLP_GROUPS, GMLP_CHUNK), 1.0)
            mix_width = GMLP_WIDTH
        elif kind == "swa":
            inp[p + "mix_win"] = nrm((D_MODEL, SWA_WIDTH + 2 * SWA_KV_WIDTH), D_MODEL ** -0.5)
            inp[p + "swa_sinks"] = nrm((SWA_Q_HEADS,), 0.5)
            mix_width = SWA_WIDTH
        else:
            inp[p + "mix_win"] = nrm((D_MODEL, 3 * FOX_WIDTH + FOX_HEADS), D_MODEL ** -0.5)
            inp[p + "fox_bf"] = 1.0 + 0.5 * nrm((FOX_HEADS,), 1.0)
            mix_width = FOX_WIDTH
        inp[p + "mix_wout"] = nrm((mix_width, D_MODEL), mix_width ** -0.5)
        inp[p + "ffn2_norm"] = gain(D_MODEL)
        inp[p + "ffn2_wi"] = nrm((D_MODEL, 2 * D_FF), D_MODEL ** -0.5)
        inp[p + "ffn2_wo"] = nrm((D_FF, D_MODEL), D_FF ** -0.5)
    inp["final_norm"] = gain(D_MODEL)
    return inp


def reference(x,
              l0_ffn1_norm, l0_ffn1_wi, l0_ffn1_wo, l0_mix_norm, l0_mix_win,
              l0_gmlp_vnorm, l0_gmlp_ws, l0_gmlp_bs, l0_mix_wout,
              l0_ffn2_norm, l0_ffn2_wi, l0_ffn2_wo,
              l1_ffn1_norm, l1_ffn1_wi, l1_ffn1_wo, l1_mix_norm, l1_mix_win,
              l1_swa_sinks, l1_mix_wout,
              l1_ffn2_norm, l1_ffn2_wi, l1_ffn2_wo,
              l2_ffn1_norm, l2_ffn1_wi, l2_ffn1_wo, l2_mix_norm, l2_mix_win,
              l2_fox_bf, l2_mix_wout,
              l2_ffn2_norm, l2_ffn2_wi, l2_ffn2_wo,
              l3_ffn1_norm, l3_ffn1_wi, l3_ffn1_wo, l3_mix_norm, l3_mix_win,
              l3_gmlp_vnorm, l3_gmlp_ws, l3_gmlp_bs, l3_mix_wout,
              l3_ffn2_norm, l3_ffn2_wi, l3_ffn2_wo,
              final_norm):
    ffn_params = [
        (l0_ffn1_norm, l0_ffn1_wi, l0_ffn1_wo, l0_mix_norm, l0_ffn2_norm, l0_ffn2_wi, l0_ffn2_wo),
        (l1_ffn1_norm, l1_ffn1_wi, l1_ffn1_wo, l1_mix_norm, l1_ffn2_norm, l1_ffn2_wi, l1_ffn2_wo),
        (l2_ffn1_norm, l2_ffn1_wi, l2_ffn1_wo, l2_mix_norm, l2_ffn2_norm, l2_ffn2_wi, l2_ffn2_wo),
        (l3_ffn1_norm, l3_ffn1_wi, l3_ffn1_wo, l3_mix_norm, l3_ffn2_norm, l3_ffn2_wi, l3_ffn2_wo),
    ]
    mixers = [
        functools.partial(gmlp_mixer, w_in=l0_mix_win, v_gain=l0_gmlp_vnorm, w_s=l0_gmlp_ws,
                          b_s=l0_gmlp_bs, w_out=l0_mix_wout),
        functools.partial(swa_mixer, w_in=l1_mix_win, sinks=l1_swa_sinks, w_out=l1_mix_wout),
        functools.partial(fox_mixer, w_in=l2_mix_win, b_f=l2_fox_bf, w_out=l2_mix_wout),
        functools.partial(gmlp_mixer, w_in=l3_mix_win, v_gain=l3_gmlp_vnorm, w_s=l3_gmlp_ws,
                          b_s=l3_gmlp_bs, w_out=l3_mix_wout),
    ]
    h = x
    for i in range(DEPTH):
        n1, wi1, wo1, nm, n2, wi2, wo2 = ffn_params[i]
        h = h + 0.5 * swiglu_ffn(rms_norm(h, n1), wi1, wo1)
        h = h + mixers[i](rms_norm(h, nm))
        h = h + 0.5 * swiglu_ffn(rms_norm(h, n2), wi2, wo2)
    return rms_norm(h, final_norm)
```

```python
import functools
import math

import jax
import jax.numpy as jnp
import numpy as np
from jax import lax
from jax.experimental import pallas as pl
from jax.experimental.pallas import tpu as pltpu

F32 = jnp.float32
BF16 = jnp.bfloat16

NORM_EPS = 1e-5
NEG_INF = -1e30
CHUNK = 128
GMLP_GROUPS = 16
SWA_HEAD_DIM = 64
SWA_Q_HEADS = 32
SWA_KV_HEADS = 4
SWA_Q_PER_KV = SWA_Q_HEADS // SWA_KV_HEADS
ROPE_THETA = 500000.0
ROPE_DIM = SWA_HEAD_DIM // 4
FOX_HEAD_DIM = 128
FOX_HEADS = 16

LANES = 128
V7X_VMEM_BYTES = 64 * 1024 * 1024
V7X_VMEM_USABLE_BYTES = 56 * 1024 * 1024


def _nbytes(shape, dtype):
    return math.prod(shape) * jnp.dtype(dtype).itemsize


def _compiler_params(semantics, pipelined, resident):
    need = 2 * sum(_nbytes(s, d) for s, d in pipelined) + sum(_nbytes(s, d) for s, d in resident)
    limit = min(V7X_VMEM_USABLE_BYTES, max(32 * 1024 * 1024, need + need // 4))
    return pltpu.CompilerParams(dimension_semantics=semantics, vmem_limit_bytes=limit)


def _rms(x, gain):
    return x * lax.rsqrt(jnp.mean(x * x, axis=-1, keepdims=True) + NORM_EPS) * gain


def _ffn_kernel(*refs, n_ff_tiles, final_norm):
    if final_norm:
        x_ref, g_ref, wg_ref, wu_ref, wo_ref, gf_ref, o_ref, xn_ref = refs
    else:
        x_ref, g_ref, wg_ref, wu_ref, wo_ref, o_ref, xn_ref = refs
    j = pl.program_id(1)

    @pl.when(j == 0)
    def _():
        xn_ref[...] = _rms(x_ref[...], g_ref[...]).astype(BF16)
        o_ref[...] = jnp.zeros_like(o_ref)

    xn = xn_ref[...]
    gate = jnp.dot(xn, wg_ref[...], preferred_element_type=F32)
    up = jnp.dot(xn, wu_ref[...], preferred_element_type=F32)
    act = (gate * jax.nn.sigmoid(gate) * up).astype(BF16)
    o_ref[...] += jnp.dot(act, wo_ref[...], preferred_element_type=F32)

    @pl.when(j == n_ff_tiles - 1)
    def _():
        h = x_ref[...] + 0.5 * o_ref[...]
        if final_norm:
            h = _rms(h, gf_ref[...])
        o_ref[...] = h


def _ffn(h, gain, wi, wo, final_gain=None, *, tm=512, tf=512):
    t, d = h.shape
    d_ff = wo.shape[0]
    n_ff_tiles = d_ff // tf
    assert t % tm == 0 and d_ff % tf == 0
    in_specs = [
        pl.BlockSpec((tm, d), lambda i, j: (i, 0)),
        pl.BlockSpec((1, d), lambda i, j: (0, 0)),
        pl.BlockSpec((d, tf), lambda i, j: (0, j)),
        pl.BlockSpec((d, tf), lambda i, j: (0, j + n_ff_tiles)),
        pl.BlockSpec((tf, d), lambda i, j: (j, 0)),
    ]
    args = [h, gain.reshape(1, d), wi, wi, wo]
    if final_gain is not None:
        in_specs.append(pl.BlockSpec((1, d), lambda i, j: (0, 0)))
        args.append(final_gain.reshape(1, d))
    return pl.pallas_call(
        functools.partial(_ffn_kernel, n_ff_tiles=n_ff_tiles, final_norm=final_gain is not None),
        grid=(t // tm, n_ff_tiles),
        in_specs=in_specs,
        out_specs=pl.BlockSpec((tm, d), lambda i, j: (i, 0)),
        out_shape=jax.ShapeDtypeStruct((t, d), F32),
        scratch_shapes=[pltpu.VMEM((tm, d), BF16)],
        compiler_params=_compiler_params(
            ("parallel", "arbitrary"),
            pipelined=[((tm, d), F32), ((tm, d), F32), ((d, tf), BF16), ((d, tf), BF16), ((tf, d), BF16)],
            resident=[((tm, d), BF16), ((tm, tf), F32), ((tm, tf), F32), ((tm, tf), F32)]),
        name="ffn",
    )(*args)


def _norm_matmul_kernel(*refs, gelu, side):
    if side:
        x_ref, g_ref, w_ref, w2_ref, o_ref, o2_ref, xn_ref = refs
    else:
        x_ref, g_ref, w_ref, o_ref, xn_ref = refs
    j = pl.program_id(1)

    @pl.when(j == 0)
    def _():
        xn = _rms(x_ref[...], g_ref[...]).astype(BF16)
        xn_ref[...] = xn
        if side:
            o2_ref[...] = jnp.dot(xn, w2_ref[...], preferred_element_type=F32)

    y = jnp.dot(xn_ref[...], w_ref[...], preferred_element_type=F32)
    if gelu:
        y = 0.5 * y * (1.0 + lax.erf(y * np.float32(math.sqrt(0.5))))
    o_ref[...] = y.astype(o_ref.dtype)


def _norm_matmul(h, gain, w, w_side=None, *, gelu=False, tm=512, tn=1024):
    t, d = h.shape
    n = w.shape[1]
    assert t % tm == 0 and n % tn == 0
    side = w_side is not None
    in_specs = [
        pl.BlockSpec((tm, d), lambda i, j: (i, 0)),
        pl.BlockSpec((1, d), lambda i, j: (0, 0)),
        pl.BlockSpec((d, tn), lambda i, j: (0, j)),
    ]
    args = [h, gain.reshape(1, d), w]
    out_specs = pl.BlockSpec((tm, tn), lambda i, j: (i, j))
    out_shape = jax.ShapeDtypeStruct((t, n), BF16)
    pipelined = [((tm, d), F32), ((d, tn), BF16), ((tm, tn), BF16)]
    if side:
        n2 = w_side.shape[1]
        in_specs.append(pl.BlockSpec((d, n2), lambda i, j: (0, 0)))
        args.append(w_side)
        out_specs = [out_specs, pl.BlockSpec((tm, n2), lambda i, j: (i, 0))]
        out_shape = [out_shape, jax.ShapeDtypeStruct((t, n2), F32)]
        pipelined += [((d, n2), BF16), ((tm, n2), F32)]
    return pl.pallas_call(
        functools.partial(_norm_matmul_kernel, gelu=gelu, side=side),
        grid=(t // tm, n // tn),
        in_specs=in_specs,
        out_specs=out_specs,
        out_shape=out_shape,
        scratch_shapes=[pltpu.VMEM((tm, d), BF16)],
        compiler_params=_compiler_params(
            ("parallel", "arbitrary"), pipelined=pipelined,
            resident=[((tm, d), BF16), ((tm, tn), F32), ((tm, tn), F32)]),
        name="norm_matmul",
    )(*args)


def _matmul_residual_kernel(y_ref, w_ref, h_ref, o_ref):
    o_ref[...] = h_ref[...] + jnp.dot(y_ref[...], w_ref[...], preferred_element_type=F32)


def _matmul_residual(y, w, h, *, tm=512):
    t, k = y.shape
    d = w.shape[1]
    assert t % tm == 0
    return pl.pallas_call(
        _matmul_residual_kernel,
        grid=(t // tm,),
        in_specs=[
            pl.BlockSpec((tm, k), lambda i: (i, 0)),
            pl.BlockSpec((k, d), lambda i: (0, 0)),
            pl.BlockSpec((tm, d), lambda i: (i, 0)),
        ],
        out_specs=pl.BlockSpec((tm, d), lambda i: (i, 0)),
        out_shape=jax.ShapeDtypeStruct((t, d), F32),
        compiler_params=_compiler_params(
            ("parallel",),
            pipelined=[((tm, k), BF16), ((k, d), BF16), ((tm, d), F32), ((tm, d), F32)],
            resident=[((tm, d), F32)]),
        name="matmul_residual",
    )(y, w, h)


def _gmlp_mix_kernel(u_ref, v_ref, vg_ref, ws_ref, bias_ref, wout_ref, h_ref, o_ref, wsm_ref, y_ref,
                     *, n_chunks):
    @pl.when(pl.program_id(0) == 0)
    def _():
        row = lax.broadcasted_iota(jnp.int32, (CHUNK, CHUNK), 0)
        col = lax.broadcasted_iota(jnp.int32, (CHUNK, CHUNK), 1)
        for g in range(GMLP_GROUPS):
            wsm_ref[g] = jnp.where(col <= row, ws_ref[g], 0.0).astype(BF16)

    for c in range(n_chunks):
        rows = slice(c * CHUNK, (c + 1) * CHUNK)
        vn = _rms(v_ref[rows, :].astype(F32), vg_ref[...]).astype(BF16)
        for g in range(GMLP_GROUPS):
            cols = slice(g * LANES, (g + 1) * LANES)
            mixed = jnp.dot(wsm_ref[g], vn[:, cols], preferred_element_type=F32) + bias_ref[:, cols]
            y_ref[rows, cols] = (u_ref[rows, cols].astype(F32) * mixed).astype(BF16)

    o_ref[...] = h_ref[...] + jnp.dot(y_ref[...], wout_ref[...], preferred_element_type=F32)


def _gmlp_mix(z, v_gain, ws, bs, w_out, h, *, tm=256):
    t, d = h.shape
    width = z.shape[1] // 2
    assert width == GMLP_GROUPS * LANES and t % tm == 0 and tm % CHUNK == 0
    n_width_blocks = 1
    bias = jnp.repeat(bs.T, LANES, axis=1)
    return pl.pallas_call(
        functools.partial(_gmlp_mix_kernel, n_chunks=tm // CHUNK),
        grid=(t // tm,),
        in_specs=[
            pl.BlockSpec((tm, width), lambda i: (i, 0)),
            pl.BlockSpec((tm, width), lambda i: (i, n_width_blocks)),
            pl.BlockSpec((1, width), lambda i: (0, 0)),
            pl.BlockSpec((GMLP_GROUPS, CHUNK, CHUNK), lambda i: (0, 0, 0)),
            pl.BlockSpec((CHUNK, width), lambda i: (0, 0)),
            pl.BlockSpec((width, d), lambda i: (0, 0)),
            pl.BlockSpec((tm, d), lambda i: (i, 0)),
        ],
        out_specs=pl.BlockSpec((tm, d), lambda i: (i, 0)),
        out_shape=jax.ShapeDtypeStruct((t, d), F32),
        scratch_shapes=[pltpu.VMEM((GMLP_GROUPS, CHUNK, CHUNK), BF16), pltpu.VMEM((tm, width), BF16)],
        compiler_params=_compiler_params(
            ("arbitrary",),
            pipelined=[((tm, width), BF16), ((tm, width), BF16), ((1, width), F32),
                       ((GMLP_GROUPS, CHUNK, CHUNK), F32), ((CHUNK, width), F32), ((width, d), BF16),
                       ((tm, d), F32), ((tm, d), F32)],
            resident=[((GMLP_GROUPS, CHUNK, CHUNK), BF16), ((tm, width), BF16), ((tm, d), F32)]),
        name="gmlp_mix",
    )(z, z, v_gain.reshape(1, width), ws, bias, w_out, h)


def _rope_tables(seq):
    half = ROPE_DIM // 2
    inv_freq = ROPE_THETA ** (-(jnp.arange(half, dtype=F32) * 2.0 / ROPE_DIM))
    ang = jnp.arange(seq, dtype=F32)[:, None] * inv_freq[None, :]
    cos, sin = jnp.cos(ang), jnp.sin(ang)
    zeros = jnp.zeros((seq, SWA_HEAD_DIM - ROPE_DIM), F32)
    cos_h = jnp.concatenate([cos, cos, zeros + 1.0], axis=1)
    sin_hi = jnp.concatenate([jnp.zeros_like(sin), sin, zeros], axis=1)
    sin_lo = jnp.concatenate([-sin, jnp.zeros_like(sin), zeros], axis=1)
    return tuple(jnp.concatenate([a, a], axis=1) for a in (cos_h, sin_hi, sin_lo))


def _swa_kernel(sinks_ref, q_ref, kc_ref, kp_ref, vc_ref, vp_ref, cos_ref, shi_ref, slo_ref, o_ref):
    n = pl.program_id(1)
    cur = pl.ds(pl.multiple_of(n * CHUNK, CHUNK), CHUNK)
    prev = pl.ds(pl.multiple_of(jnp.maximum(n - 1, 0) * CHUNK, CHUNK), CHUNK)

    def rope(x, rows):
        return (x * cos_ref[rows, :] + pltpu.roll(x, ROPE_DIM // 2, axis=1) * shi_ref[rows, :]
                + pltpu.roll(x, LANES - ROPE_DIM // 2, axis=1) * slo_ref[rows, :])

    row = lax.broadcasted_iota(jnp.int32, (2 * CHUNK, 2 * CHUNK), 0) % CHUNK
    col = lax.broadcasted_iota(jnp.int32, (2 * CHUNK, 2 * CHUNK), 1)
    first_key = jnp.where(n > 0, 0, CHUNK)
    valid = (col > row) & (col <= row + CHUNK) & (col >= first_key)
    first_head = lax.broadcasted_iota(jnp.int32, (2 * CHUNK, 1), 0) < CHUNK
    low_lanes = lax.broadcasted_iota(jnp.int32, (CHUNK, LANES), 1) < SWA_HEAD_DIM

    for g in range(SWA_KV_HEADS):
        lanes_g = slice(g * LANES, (g + 1) * LANES)
        k = jnp.concatenate([rope(kp_ref[0, :, lanes_g].astype(F32), prev),
                             rope(kc_ref[0, :, lanes_g].astype(F32), cur)], axis=0).astype(BF16)
        v = jnp.concatenate([vp_ref[0, :, lanes_g], vc_ref[0, :, lanes_g]], axis=0)
        for pair in range(SWA_Q_PER_KV // 2):
            idx = g * (SWA_Q_PER_KV // 2) + pair
            lanes_q = slice(idx * LANES, (idx + 1) * LANES)
            q = rope(q_ref[0, :, lanes_q].astype(F32), cur) * np.float32(SWA_HEAD_DIM ** -0.5)
            q2 = jnp.concatenate([jnp.where(low_lanes, q, 0.0), jnp.where(low_lanes, 0.0, q)],
                                 axis=0).astype(BF16)
            s = lax.dot_general(q2, k, (((1,), (1,)), ((), ())), preferred_element_type=F32)
            s = jnp.where(valid, s, NEG_INF)
            sink = jnp.where(first_head, sinks_ref[2 * idx], sinks_ref[2 * idx + 1])
            m = jnp.maximum(jnp.max(s, axis=1, keepdims=True), sink)
            p = jnp.exp(s - m)
            denom = jnp.sum(p, axis=1, keepdims=True) + jnp.exp(sink - m)
            o2 = jnp.dot(p.astype(BF16), v, preferred_element_type=F32) / denom
            o_ref[0, :, lanes_q] = jnp.where(low_lanes, o2[:CHUNK], o2[CHUNK:]).astype(o_ref.dtype)


def _swa_attention(qkv, sinks, batch, seq):
    nb = seq // CHUNK
    qw = SWA_Q_HEADS * SWA_HEAD_DIM
    kw = SWA_KV_HEADS * LANES
    qkv = qkv.reshape(batch, seq, qw + 2 * kw)
    cos, sin_hi, sin_lo = _rope_tables(seq)
    k_blk = qw // kw
    table_spec = pl.BlockSpec((seq, LANES), lambda b, n: (0, 0))
    return pl.pallas_call(
        _swa_kernel,
        grid=(batch, nb),
        in_specs=[
            pl.BlockSpec(memory_space=pltpu.SMEM),
            pl.BlockSpec((1, CHUNK, qw), lambda b, n: (b, n, 0)),
            pl.BlockSpec((1, CHUNK, kw), lambda b, n: (b, n, k_blk)),
            pl.BlockSpec((1, CHUNK, kw), lambda b, n: (b, jnp.maximum(n - 1, 0), k_blk)),
            pl.BlockSpec((1, CHUNK, kw), lambda b, n: (b, n, k_blk + 1)),
            pl.BlockSpec((1, CHUNK, kw), lambda b, n: (b, jnp.maximum(n - 1, 0), k_blk + 1)),
            table_spec, table_spec, table_spec,
        ],
        out_specs=pl.BlockSpec((1, CHUNK, qw), lambda b, n: (b, n, 0)),
        out_shape=jax.ShapeDtypeStruct((batch, seq, qw), BF16),
        compiler_params=_compiler_params(
            ("parallel", "arbitrary"),
            pipelined=[((CHUNK, qw), BF16)] * 2 + [((CHUNK, kw), BF16)] * 4 + [((seq, LANES), F32)] * 3,
            resident=[((2 * CHUNK, 2 * CHUNK), F32)] * 8),
        name="swa_attention",
    )(sinks, qkv, qkv, qkv, qkv, qkv, cos, sin_hi, sin_lo).reshape(batch * seq, qw)


FOX_CUMSUM_CHUNK = 256


def _fox_decay_kernel(fl_ref, bf_ref, o_ref, *, n_chunks):
    cc = FOX_CUMSUM_CHUNK
    row = lax.broadcasted_iota(jnp.int32, (cc, cc), 0)
    col = lax.broadcasted_iota(jnp.int32, (cc, cc), 1)
    lower = jnp.where(col <= row, 1.0, 0.0).astype(BF16)
    carry = jnp.zeros((1, LANES), F32)
    for c in range(n_chunks):
        rows = slice(c * cc, (c + 1) * cc)
        x = jax.nn.log_sigmoid(fl_ref[0, rows, :] + bf_ref[...])
        hi = x.astype(BF16)
        rest = x - hi.astype(F32)
        mid = rest.astype(BF16)
        lo = (rest - mid.astype(F32)).astype(BF16)
        cs = (jnp.dot(lower, hi, preferred_element_type=F32) + jnp.dot(lower, mid, preferred_element_type=F32)
              + jnp.dot(lower, lo, preferred_element_type=F32)) + carry
        o_ref[0, rows, :] = cs
        carry = cs[cc - 1:cc, :]


def _fox_decay(fl, b_f, batch, seq):
    assert seq % FOX_CUMSUM_CHUNK == 0
    bf = jnp.zeros((1, LANES), F32).at[0, :FOX_HEADS].set(b_f)
    return pl.pallas_call(
        functools.partial(_fox_decay_kernel, n_chunks=seq // FOX_CUMSUM_CHUNK),
        grid=(batch,),
        in_specs=[pl.BlockSpec((1, seq, LANES), lambda b: (b, 0, 0)), pl.BlockSpec((1, LANES), lambda b: (0, 0))],
        out_specs=pl.BlockSpec((1, seq, LANES), lambda b: (b, 0, 0)),
        out_shape=jax.ShapeDtypeStruct((batch, seq, LANES), F32),
        compiler_params=_compiler_params(
            ("parallel",), pipelined=[((seq, LANES), F32)] * 2, resident=[((seq, LANES), F32)]),
        name="fox_decay",
    )(fl.reshape(batch, seq, LANES), bf)


def _fox_kernel(q_ref, k_ref, v_ref, dq_ref, dk_ref, o_ref, *, tq):
    head = pl.program_id(1)
    i = pl.program_id(2)
    lane = lax.broadcasted_iota(jnp.int32, (1, LANES), 1)
    dq = jnp.sum(jnp.where(lane == head, dq_ref[0], 0.0), axis=1, keepdims=True)
    q = q_ref[0]
    scale = np.float32(FOX_HEAD_DIM ** -0.5)

    def scores(kt):
        k = k_ref[0, pl.ds(pl.multiple_of(kt * tq, tq), tq), :]
        s = lax.dot_general(q, k, (((1,), (1,)), ((), ())), preferred_element_type=F32) * scale
        return s + dq - dk_ref[0, 0, pl.ds(kt, 1), :]

    def update(kt, s, carry):
        m, l, acc = carry
        v = v_ref[0, pl.ds(pl.multiple_of(kt * tq, tq), tq), :]
        m_new = jnp.maximum(m, jnp.max(s, axis=1, keepdims=True))
        alpha = jnp.exp(m - m_new)
        p = jnp.exp(s - m_new)
        l = alpha * l + jnp.sum(p, axis=1, keepdims=True)
        acc = alpha * acc + jnp.dot(p.astype(BF16), v, preferred_element_type=F32)
        return m_new, l, acc

    init = (jnp.full((tq, 1), NEG_INF, F32), jnp.zeros((tq, 1), F32), jnp.zeros((tq, FOX_HEAD_DIM), F32))
    carry = lax.fori_loop(0, i, lambda kt, c: update(kt, scores(kt), c), init)
    row = lax.broadcasted_iota(jnp.int32, (tq, tq), 0)
    col = lax.broadcasted_iota(jnp.int32, (tq, tq), 1)
    _, l, acc = update(i, jnp.where(col <= row, scores(i), NEG_INF), carry)
    o_ref[0] = (acc / l).astype(o_ref.dtype)


def _fox_attention(qkv, dec, batch, seq, *, tq=256):
    width = FOX_HEADS * FOX_HEAD_DIM
    assert seq % tq == 0
    qkv = qkv.reshape(batch, seq, 3 * width)
    dec_k = dec[:, :, :FOX_HEADS].transpose(0, 2, 1).reshape(batch, FOX_HEADS, seq // tq, tq)
    return pl.pallas_call(
        functools.partial(_fox_kernel, tq=tq),
        grid=(batch, FOX_HEADS, seq // tq),
        in_specs=[
            pl.BlockSpec((1, tq, FOX_HEAD_DIM), lambda b, h, i: (b, i, h)),
            pl.BlockSpec((1, seq, FOX_HEAD_DIM), lambda b, h, i: (b, 0, FOX_HEADS + h)),
            pl.BlockSpec((1, seq, FOX_HEAD_DIM), lambda b, h, i: (b, 0, 2 * FOX_HEADS + h)),
            pl.BlockSpec((1, tq, LANES), lambda b, h, i: (b, i, 0)),
            pl.BlockSpec((1, 1, seq // tq, tq), lambda b, h, i: (b, h, 0, 0)),
        ],
        out_specs=pl.BlockSpec((1, tq, FOX_HEAD_DIM), lambda b, h, i: (b, i, h)),
        out_shape=jax.ShapeDtypeStruct((batch, seq, width), BF16),
        compiler_params=_compiler_params(
            ("parallel", "parallel", "arbitrary"),
            pipelined=[((tq, FOX_HEAD_DIM), BF16)] * 2 + [((seq, FOX_HEAD_DIM), BF16)] * 2
                      + [((tq, LANES), F32), ((seq // tq, tq), F32)],
            resident=[((tq, tq), F32)] * 6),
        name="fox_attention",
    )(qkv, qkv, qkv, dec, dec_k).reshape(batch * seq, width)


def _gmlp_layer(h, norm, w_in, v_gain, ws, bs, w_out):
    z = _norm_matmul(h, norm, w_in.astype(BF16), gelu=True)
    return _gmlp_mix(z, v_gain, ws, bs, w_out.astype(BF16), h)


def _swa_layer(h, norm, w_in, sinks, w_out, batch, seq):
    d = h.shape[1]
    qw = SWA_Q_HEADS * SWA_HEAD_DIM
    kvw = SWA_KV_HEADS * SWA_HEAD_DIM

    def duplicate_heads(w):
        w = w.reshape(d, SWA_KV_HEADS, 1, SWA_HEAD_DIM)
        return jnp.broadcast_to(w, (d, SWA_KV_HEADS, 2, SWA_HEAD_DIM)).reshape(d, SWA_KV_HEADS * LANES)

    w = jnp.concatenate([w_in[:, :qw], duplicate_heads(w_in[:, qw:qw + kvw]),
                         duplicate_heads(w_in[:, qw + kvw:])], axis=1).astype(BF16)
    qkv = _norm_matmul(h, norm, w)
    o = _swa_attention(qkv, sinks, batch, seq)
    return _matmul_residual(o, w_out.astype(BF16), h)


def _fox_layer(h, norm, w_in, b_f, w_out, batch, seq):
    d = h.shape[1]
    width = FOX_HEADS * FOX_HEAD_DIM
    w_f = jnp.zeros((d, LANES), BF16).at[:, :FOX_HEADS].set(w_in[:, 3 * width:].astype(BF16))
    qkv, fl = _norm_matmul(h, norm, w_in[:, :3 * width].astype(BF16), w_f)
    dec = _fox_decay(fl, b_f, batch, seq)
    o = _fox_attention(qkv, dec, batch, seq)
    return _matmul_residual(o, w_out.astype(BF16), h)


def kernel(x, l0_ffn1_norm, l0_ffn1_wi, l0_ffn1_wo, l0_mix_norm, l0_mix_win, l0_gmlp_vnorm, l0_gmlp_ws, l0_gmlp_bs, l0_mix_wout, l0_ffn2_norm, l0_ffn2_wi, l0_ffn2_wo, l1_ffn1_norm, l1_ffn1_wi, l1_ffn1_wo, l1_mix_norm, l1_mix_win, l1_swa_sinks, l1_mix_wout, l1_ffn2_norm, l1_ffn2_wi, l1_ffn2_wo, l2_ffn1_norm, l2_ffn1_wi, l2_ffn1_wo, l2_mix_norm, l2_mix_win, l2_fox_bf, l2_mix_wout, l2_ffn2_norm, l2_ffn2_wi, l2_ffn2_wo, l3_ffn1_norm, l3_ffn1_wi, l3_ffn1_wo, l3_mix_norm, l3_mix_win, l3_gmlp_vnorm, l3_gmlp_ws, l3_gmlp_bs, l3_mix_wout, l3_ffn2_norm, l3_ffn2_wi, l3_ffn2_wo, final_norm):
    batch, seq, d = x.shape
    h = x.reshape(batch * seq, d)

    def ffn(h, norm, wi, wo, final_gain=None):
        return _ffn(h, norm, wi.astype(BF16), wo.astype(BF16), final_gain)

    h = ffn(h, l0_ffn1_norm, l0_ffn1_wi, l0_ffn1_wo)
    h = _gmlp_layer(h, l0_mix_norm, l0_mix_win, l0_gmlp_vnorm, l0_gmlp_ws, l0_gmlp_bs, l0_mix_wout)
    h = ffn(h, l0_ffn2_norm, l0_ffn2_wi, l0_ffn2_wo)

    h = ffn(h, l1_ffn1_norm, l1_ffn1_wi, l1_ffn1_wo)
    h = _swa_layer(h, l1_mix_norm, l1_mix_win, l1_swa_sinks, l1_mix_wout, batch, seq)
    h = ffn(h, l1_ffn2_norm, l1_ffn2_wi, l1_ffn2_wo)

    h = ffn(h, l2_ffn1_norm, l2_ffn1_wi, l2_ffn1_wo)
    h = _fox_layer(h, l2_mix_norm, l2_mix_win, l2_fox_bf, l2_mix_wout, batch, seq)
    h = ffn(h, l2_ffn2_norm, l2_ffn2_wi, l2_ffn2_wo)

    h = ffn(h, l3_ffn1_norm, l3_ffn1_wi, l3_ffn1_wo)
    h = _gmlp_layer(h, l3_mix_norm, l3_mix_win, l3_gmlp_vnorm, l3_gmlp_ws, l3_gmlp_bs, l3_mix_wout)
    h = ffn(h, l3_ffn2_norm, l3_ffn2_wi, l3_ffn2_wo, final_norm)
    return h.reshape(batch, seq, d)
```

```python
import functools
import math

import jax
import jax.numpy as jnp
import numpy as np
from jax import lax
from jax.experimental import pallas as pl
from jax.experimental.pallas import tpu as pltpu

F32 = jnp.float32
BF16 = jnp.bfloat16

NORM_EPS = 1e-5
NEG_INF = -1e30
CHUNK = 128
GMLP_GROUPS = 16
SWA_HEAD_DIM = 64
SWA_Q_HEADS = 32
SWA_KV_HEADS = 4
SWA_Q_PER_KV = SWA_Q_HEADS // SWA_KV_HEADS
ROPE_THETA = 500000.0
ROPE_DIM = SWA_HEAD_DIM // 4
FOX_HEAD_DIM = 128
FOX_HEADS = 16

LANES = 128
V7X_VMEM_BYTES = 64 * 1024 * 1024
V7X_VMEM_USABLE_BYTES = 56 * 1024 * 1024


def _nbytes(shape, dtype):
    return math.prod(shape) * jnp.dtype(dtype).itemsize


def _compiler_params(semantics, pipelined, resident):
    need = 2 * sum(_nbytes(s, d) for s, d in pipelined) + sum(_nbytes(s, d) for s, d in resident)
    limit = min(V7X_VMEM_USABLE_BYTES, max(32 * 1024 * 1024, need + need // 4))
    return pltpu.CompilerParams(dimension_semantics=semantics, vmem_limit_bytes=limit)


def _rms(x, gain):
    return x * lax.rsqrt(jnp.mean(x * x, axis=-1, keepdims=True) + NORM_EPS) * gain


def _ffn_kernel(*refs, n_ff_tiles, final_norm):
    if final_norm:
        x_ref, g_ref, wg_ref, wu_ref, wo_ref, gf_ref, o_ref, xn_ref = refs
    else:
        x_ref, g_ref, wg_ref, wu_ref, wo_ref, o_ref, xn_ref = refs
    j = pl.program_id(1)

    @pl.when(j == 0)
    def _():
        x = x_ref[...]
        xn_ref[...] = _rms(x, g_ref[...]).astype(BF16)
        o_ref[...] = x

    xn = xn_ref[...]
    gate = jnp.dot(xn, wg_ref[...], preferred_element_type=F32)
    up = jnp.dot(xn, wu_ref[...], preferred_element_type=F32)
    act = (gate * jax.nn.sigmoid(gate) * up).astype(BF16)
    o_ref[...] += jnp.dot(act, wo_ref[...], preferred_element_type=F32)

    if final_norm:
        @pl.when(j == n_ff_tiles - 1)
        def _():
            o_ref[...] = _rms(o_ref[...], gf_ref[...])


def _ffn(h, gain, wi, wo_half, final_gain=None, *, tm=1024, tf=512):
    wo = wo_half
    t, d = h.shape
    tm = min(tm, t)
    d_ff = wo.shape[0]
    n_ff_tiles = d_ff // tf
    assert t % tm == 0 and d_ff % tf == 0
    in_specs = [
        pl.BlockSpec((tm, d), lambda i, j: (i, 0)),
        pl.BlockSpec((1, d), lambda i, j: (0, 0)),
        pl.BlockSpec((d, tf), lambda i, j: (0, j)),
        pl.BlockSpec((d, tf), lambda i, j: (0, j + n_ff_tiles)),
        pl.BlockSpec((tf, d), lambda i, j: (j, 0)),
    ]
    args = [h, gain.reshape(1, d), wi, wi, wo]
    if final_gain is not None:
        in_specs.append(pl.BlockSpec((1, d), lambda i, j: (0, 0)))
        args.append(final_gain.reshape(1, d))
    return pl.pallas_call(
        functools.partial(_ffn_kernel, n_ff_tiles=n_ff_tiles, final_norm=final_gain is not None),
        grid=(t // tm, n_ff_tiles),
        in_specs=in_specs,
        out_specs=pl.BlockSpec((tm, d), lambda i, j: (i, 0)),
        out_shape=jax.ShapeDtypeStruct((t, d), F32),
        scratch_shapes=[pltpu.VMEM((tm, d), BF16)],
        compiler_params=_compiler_params(
            ("parallel", "arbitrary"),
            pipelined=[((tm, d), F32), ((tm, d), F32), ((d, tf), BF16), ((d, tf), BF16), ((tf, d), BF16)],
            resident=[((tm, d), BF16), ((tm, tf), F32), ((tm, tf), F32), ((tm, tf), F32)]),
        name="ffn",
    )(*args)


def _norm_matmul_kernel(*refs, gelu, side):
    if side:
        x_ref, g_ref, w_ref, w2_ref, o_ref, o2_ref, xn_ref = refs
    else:
        x_ref, g_ref, w_ref, o_ref, xn_ref = refs
    j = pl.program_id(1)

    @pl.when(j == 0)
    def _():
        xn = _rms(x_ref[...], g_ref[...]).astype(BF16)
        xn_ref[...] = xn
        if side:
            o2_ref[...] = jnp.dot(xn, w2_ref[...], preferred_element_type=F32)

    y = jnp.dot(xn_ref[...], w_ref[...], preferred_element_type=F32)
    if gelu:
        y = 0.5 * y * (1.0 + lax.erf(y * np.float32(math.sqrt(0.5))))
    o_ref[...] = y.astype(o_ref.dtype)


def _norm_matmul(h, gain, w, w_side=None, *, gelu=False, tm=1024, tn=1024):
    t, d = h.shape
    tm = min(tm, t)
    n = w.shape[1]
    assert t % tm == 0 and n % tn == 0
    side = w_side is not None
    in_specs = [
        pl.BlockSpec((tm, d), lambda i, j: (i, 0)),
        pl.BlockSpec((1, d), lambda i, j: (0, 0)),
        pl.BlockSpec((d, tn), lambda i, j: (0, j)),
    ]
    args = [h, gain.reshape(1, d), w]
    out_specs = pl.BlockSpec((tm, tn), lambda i, j: (i, j))
    out_shape = jax.ShapeDtypeStruct((t, n), BF16)
    pipelined = [((tm, d), F32), ((d, tn), BF16), ((tm, tn), BF16)]
    if side:
        n2 = w_side.shape[1]
        in_specs.append(pl.BlockSpec((d, n2), lambda i, j: (0, 0)))
        args.append(w_side)
        out_specs = [out_specs, pl.BlockSpec((tm, n2), lambda i, j: (i, 0))]
        out_shape = [out_shape, jax.ShapeDtypeStruct((t, n2), F32)]
        pipelined += [((d, n2), BF16), ((tm, n2), F32)]
    return pl.pallas_call(
        functools.partial(_norm_matmul_kernel, gelu=gelu, side=side),
        grid=(t // tm, n // tn),
        in_specs=in_specs,
        out_specs=out_specs,
        out_shape=out_shape,
        scratch_shapes=[pltpu.VMEM((tm, d), BF16)],
        compiler_params=_compiler_params(
            ("parallel", "arbitrary"), pipelined=pipelined,
            resident=[((tm, d), BF16), ((tm, tn), F32), ((tm, tn), F32)]),
        name="norm_matmul",
    )(*args)


def _matmul_residual_kernel(y_ref, w_ref, h_ref, o_ref):
    o_ref[...] = h_ref[...] + jnp.dot(y_ref[...], w_ref[...], preferred_element_type=F32)


def _matmul_residual(y, w, h, *, tm=512):
    t, k = y.shape
    d = w.shape[1]
    assert t % tm == 0
    return pl.pallas_call(
        _matmul_residual_kernel,
        grid=(t // tm,),
        in_specs=[
            pl.BlockSpec((tm, k), lambda i: (i, 0)),
            pl.BlockSpec((k, d), lambda i: (0, 0)),
            pl.BlockSpec((tm, d), lambda i: (i, 0)),
        ],
        out_specs=pl.BlockSpec((tm, d), lambda i: (i, 0)),
        out_shape=jax.ShapeDtypeStruct((t, d), F32),
        compiler_params=_compiler_params(
            ("parallel",),
            pipelined=[((tm, k), BF16), ((k, d), BF16), ((tm, d), F32), ((tm, d), F32)],
            resident=[((tm, d), F32)]),
        name="matmul_residual",
    )(y, w, h)


def _gmlp_mix_kernel(u_ref, v_ref, vg_ref, ws_ref, bias_ref, wout_ref, h_ref, o_ref, wsm_ref, y_ref,
                     *, n_chunks):
    @pl.when(pl.program_id(0) == 0)
    def _():
        row = lax.broadcasted_iota(jnp.int32, (CHUNK, CHUNK), 0)
        col = lax.broadcasted_iota(jnp.int32, (CHUNK, CHUNK), 1)
        for g in range(GMLP_GROUPS):
            wsm_ref[g] = jnp.where(col <= row, ws_ref[g], 0.0).astype(BF16)

    for c in range(n_chunks):
        rows = slice(c * CHUNK, (c + 1) * CHUNK)
        vn = _rms(v_ref[rows, :].astype(F32), vg_ref[...]).astype(BF16)
        for g in range(GMLP_GROUPS):
            cols = slice(g * LANES, (g + 1) * LANES)
            mixed = jnp.dot(wsm_ref[g], vn[:, cols], preferred_element_type=F32) + bias_ref[:, cols]
            y_ref[rows, cols] = (u_ref[rows, cols].astype(F32) * mixed).astype(BF16)

    o_ref[...] = h_ref[...] + jnp.dot(y_ref[...], wout_ref[...], preferred_element_type=F32)


def _gmlp_mix(z, v_gain, ws, bs, w_out, h, *, tm=256):
    t, d = h.shape
    width = z.shape[1] // 2
    assert width == GMLP_GROUPS * LANES and t % tm == 0 and tm % CHUNK == 0
    n_width_blocks = 1
    bias = jnp.repeat(bs.T, LANES, axis=1)
    return pl.pallas_call(
        functools.partial(_gmlp_mix_kernel, n_chunks=tm // CHUNK),
        grid=(t // tm,),
        in_specs=[
            pl.BlockSpec((tm, width), lambda i: (i, 0)),
            pl.BlockSpec((tm, width), lambda i: (i, n_width_blocks)),
            pl.BlockSpec((1, width), lambda i: (0, 0)),
            pl.BlockSpec((GMLP_GROUPS, CHUNK, CHUNK), lambda i: (0, 0, 0)),
            pl.BlockSpec((CHUNK, width), lambda i: (0, 0)),
            pl.BlockSpec((width, d), lambda i: (0, 0)),
            pl.BlockSpec((tm, d), lambda i: (i, 0)),
        ],
        out_specs=pl.BlockSpec((tm, d), lambda i: (i, 0)),
        out_shape=jax.ShapeDtypeStruct((t, d), F32),
        scratch_shapes=[pltpu.VMEM((GMLP_GROUPS, CHUNK, CHUNK), BF16), pltpu.VMEM((tm, width), BF16)],
        compiler_params=_compiler_params(
            ("arbitrary",),
            pipelined=[((tm, width), BF16), ((tm, width), BF16), ((1, width), F32),
                       ((GMLP_GROUPS, CHUNK, CHUNK), F32), ((CHUNK, width), F32), ((width, d), BF16),
                       ((tm, d), F32), ((tm, d), F32)],
            resident=[((GMLP_GROUPS, CHUNK, CHUNK), BF16), ((tm, width), BF16), ((tm, d), F32)]),
        name="gmlp_mix",
    )(z, z, v_gain.reshape(1, width), ws, bias, w_out, h)


def _rope_tables(seq):
    half = ROPE_DIM // 2
    inv_freq = ROPE_THETA ** (-(jnp.arange(half, dtype=F32) * 2.0 / ROPE_DIM))
    ang = jnp.arange(seq, dtype=F32)[:, None] * inv_freq[None, :]
    cos, sin = jnp.cos(ang), jnp.sin(ang)
    zeros = jnp.zeros((seq, SWA_HEAD_DIM - ROPE_DIM), F32)
    cos_h = jnp.concatenate([cos, cos, zeros + 1.0], axis=1)
    sin_hi = jnp.concatenate([jnp.zeros_like(sin), sin, zeros], axis=1)
    sin_lo = jnp.concatenate([-sin, jnp.zeros_like(sin), zeros], axis=1)
    return tuple(jnp.concatenate([a, a], axis=1) for a in (cos_h, sin_hi, sin_lo))


def _swa_kernel(sinks_ref, q_ref, kc_ref, kp_ref, vc_ref, vp_ref, cos_ref, shi_ref, slo_ref, o_ref):
    n = pl.program_id(1)
    cur = pl.ds(pl.multiple_of(n * CHUNK, CHUNK), CHUNK)
    prev = pl.ds(pl.multiple_of(jnp.maximum(n - 1, 0) * CHUNK, CHUNK), CHUNK)

    def rope(x, rows):
        return (x * cos_ref[rows, :] + pltpu.roll(x, ROPE_DIM // 2, axis=1) * shi_ref[rows, :]
                + pltpu.roll(x, LANES - ROPE_DIM // 2, axis=1) * slo_ref[rows, :])

    row = lax.broadcasted_iota(jnp.int32, (2 * CHUNK, 2 * CHUNK), 0) % CHUNK
    col = lax.broadcasted_iota(jnp.int32, (2 * CHUNK, 2 * CHUNK), 1)
    first_key = jnp.where(n > 0, 0, CHUNK)
    valid = (col > row) & (col <= row + CHUNK) & (col >= first_key)
    first_head = lax.broadcasted_iota(jnp.int32, (2 * CHUNK, 1), 0) < CHUNK
    low_lanes = lax.broadcasted_iota(jnp.int32, (CHUNK, LANES), 1) < SWA_HEAD_DIM

    for g in range(SWA_KV_HEADS):
        lanes_g = slice(g * LANES, (g + 1) * LANES)
        k = jnp.concatenate([rope(kp_ref[0, :, lanes_g].astype(F32), prev),
                             rope(kc_ref[0, :, lanes_g].astype(F32), cur)], axis=0).astype(BF16)
        v = jnp.concatenate([vp_ref[0, :, lanes_g], vc_ref[0, :, lanes_g]], axis=0)
        for pair in range(SWA_Q_PER_KV // 2):
            idx = g * (SWA_Q_PER_KV // 2) + pair
            lanes_q = slice(idx * LANES, (idx + 1) * LANES)
            q = rope(q_ref[0, :, lanes_q].astype(F32), cur) * np.float32(SWA_HEAD_DIM ** -0.5)
            q2 = jnp.concatenate([jnp.where(low_lanes, q, 0.0), jnp.where(low_lanes, 0.0, q)],
                                 axis=0).astype(BF16)
            s = lax.dot_general(q2, k, (((1,), (1,)), ((), ())), preferred_element_type=F32)
            s = jnp.where(valid, s, NEG_INF)
            sink = jnp.where(first_head, sinks_ref[2 * idx], sinks_ref[2 * idx + 1])
            m = jnp.maximum(jnp.max(s, axis=1, keepdims=True), sink)
            p = jnp.exp(s - m)
            denom = jnp.sum(p, axis=1, keepdims=True) + jnp.exp(sink - m)
            o2 = jnp.dot(p.astype(BF16), v, preferred_element_type=F32) / denom
            o_ref[0, :, lanes_q] = jnp.where(low_lanes, o2[:CHUNK], o2[CHUNK:]).astype(o_ref.dtype)


def _swa_attention(qkv, sinks, batch, seq):
    nb = seq // CHUNK
    qw = SWA_Q_HEADS * SWA_HEAD_DIM
    kw = SWA_KV_HEADS * LANES
    qkv = qkv.reshape(batch, seq, qw + 2 * kw)
    cos, sin_hi, sin_lo = _rope_tables(seq)
    k_blk = qw // kw
    table_spec = pl.BlockSpec((seq, LANES), lambda b, n: (0, 0))
    return pl.pallas_call(
        _swa_kernel,
        grid=(batch, nb),
        in_specs=[
            pl.BlockSpec(memory_space=pltpu.SMEM),
            pl.BlockSpec((1, CHUNK, qw), lambda b, n: (b, n, 0)),
            pl.BlockSpec((1, CHUNK, kw), lambda b, n: (b, n, k_blk)),
            pl.BlockSpec((1, CHUNK, kw), lambda b, n: (b, jnp.maximum(n - 1, 0), k_blk)),
            pl.BlockSpec((1, CHUNK, kw), lambda b, n: (b, n, k_blk + 1)),
            pl.BlockSpec((1, CHUNK, kw), lambda b, n: (b, jnp.maximum(n - 1, 0), k_blk + 1)),
            table_spec, table_spec, table_spec,
        ],
        out_specs=pl.BlockSpec((1, CHUNK, qw), lambda b, n: (b, n, 0)),
        out_shape=jax.ShapeDtypeStruct((batch, seq, qw), BF16),
        compiler_params=_compiler_params(
            ("parallel", "arbitrary"),
            pipelined=[((CHUNK, qw), BF16)] * 2 + [((CHUNK, kw), BF16)] * 4 + [((seq, LANES), F32)] * 3,
            resident=[((2 * CHUNK, 2 * CHUNK), F32)] * 8),
        name="swa_attention",
    )(sinks, qkv, qkv, qkv, qkv, qkv, cos, sin_hi, sin_lo).reshape(batch * seq, qw)


FOX_CUMSUM_CHUNK = 256


def _fox_decay_kernel(fl_ref, bf_ref, o_ref, *, n_chunks):
    cc = FOX_CUMSUM_CHUNK
    row = lax.broadcasted_iota(jnp.int32, (cc, cc), 0)
    col = lax.broadcasted_iota(jnp.int32, (cc, cc), 1)
    lower = jnp.where(col <= row, 1.0, 0.0).astype(BF16)
    carry = jnp.zeros((1, LANES), F32)
    for c in range(n_chunks):
        rows = slice(c * cc, (c + 1) * cc)
        x = jax.nn.log_sigmoid(fl_ref[0, rows, :] + bf_ref[...])
        hi = x.astype(BF16)
        rest = x - hi.astype(F32)
        mid = rest.astype(BF16)
        lo = (rest - mid.astype(F32)).astype(BF16)
        cs = (jnp.dot(lower, hi, preferred_element_type=F32) + jnp.dot(lower, mid, preferred_element_type=F32)
              + jnp.dot(lower, lo, preferred_element_type=F32)) + carry
        o_ref[0, rows, :] = cs
        carry = cs[cc - 1:cc, :]


def _fox_decay(fl, b_f, batch, seq):
    assert seq % FOX_CUMSUM_CHUNK == 0
    bf = jnp.zeros((1, LANES), F32).at[0, :FOX_HEADS].set(b_f)
    return pl.pallas_call(
        functools.partial(_fox_decay_kernel, n_chunks=seq // FOX_CUMSUM_CHUNK),
        grid=(batch,),
        in_specs=[pl.BlockSpec((1, seq, LANES), lambda b: (b, 0, 0)), pl.BlockSpec((1, LANES), lambda b: (0, 0))],
        out_specs=pl.BlockSpec((1, seq, LANES), lambda b: (b, 0, 0)),
        out_shape=jax.ShapeDtypeStruct((batch, seq, LANES), F32),
        compiler_params=_compiler_params(
            ("parallel",), pipelined=[((seq, LANES), F32)] * 2, resident=[((seq, LANES), F32)]),
        name="fox_decay",
    )(fl.reshape(batch, seq, LANES), bf)


def _fox_kernel(q_ref, k_ref, v_ref, dq_ref, dk_ref, o_ref, *, tq):
    head = pl.program_id(1)
    i = pl.program_id(2)
    log2e = np.float32(math.log2(math.e))
    lane = lax.broadcasted_iota(jnp.int32, (1, LANES), 1)
    dq = jnp.sum(jnp.where(lane == head, dq_ref[0], 0.0), axis=1, keepdims=True) * log2e
    q = q_ref[0]
    scale = np.float32(FOX_HEAD_DIM ** -0.5 * math.log2(math.e))

    def scores(kt):
        k = k_ref[0, pl.ds(pl.multiple_of(kt * tq, tq), tq), :]
        s = lax.dot_general(q, k, (((1,), (1,)), ((), ())), preferred_element_type=F32) * scale
        return s + dq - dk_ref[0, 0, pl.ds(kt, 1), :] * log2e

    def update(kt, s, carry):
        m, l, acc = carry
        v = v_ref[0, pl.ds(pl.multiple_of(kt * tq, tq), tq), :]
        m_new = jnp.maximum(m, jnp.max(s, axis=1, keepdims=True))
        alpha = jnp.exp2(m - m_new)
        p = jnp.exp2(s - m_new)
        l = alpha * l + jnp.sum(p, axis=1, keepdims=True)
        acc = alpha * acc + jnp.dot(p.astype(BF16), v, preferred_element_type=F32)
        return m_new, l, acc

    init = (jnp.full((tq, 1), NEG_INF, F32), jnp.zeros((tq, 1), F32), jnp.zeros((tq, FOX_HEAD_DIM), F32))
    carry = lax.fori_loop(0, i, lambda kt, c: update(kt, scores(kt), c), init)
    row = lax.broadcasted_iota(jnp.int32, (tq, tq), 0)
    col = lax.broadcasted_iota(jnp.int32, (tq, tq), 1)
    _, l, acc = update(i, jnp.where(col <= row, scores(i), NEG_INF), carry)
    o_ref[0] = (acc / l).astype(o_ref.dtype)


def _fox_attention(qkv, dec, batch, seq, *, tq=512):
    width = FOX_HEADS * FOX_HEAD_DIM
    assert seq % tq == 0
    qkv = qkv.reshape(batch, seq, 3 * width)
    dec_k = dec[:, :, :FOX_HEADS].transpose(0, 2, 1).reshape(batch, FOX_HEADS, seq // tq, tq)
    return pl.pallas_call(
        functools.partial(_fox_kernel, tq=tq),
        grid=(batch, FOX_HEADS, seq // tq),
        in_specs=[
            pl.BlockSpec((1, tq, FOX_HEAD_DIM), lambda b, h, i: (b, i, h)),
            pl.BlockSpec((1, seq, FOX_HEAD_DIM), lambda b, h, i: (b, 0, FOX_HEADS + h)),
            pl.BlockSpec((1, seq, FOX_HEAD_DIM), lambda b, h, i: (b, 0, 2 * FOX_HEADS + h)),
            pl.BlockSpec((1, tq, LANES), lambda b, h, i: (b, i, 0)),
            pl.BlockSpec((1, 1, seq // tq, tq), lambda b, h, i: (b, h, 0, 0)),
        ],
        out_specs=pl.BlockSpec((1, tq, FOX_HEAD_DIM), lambda b, h, i: (b, i, h)),
        out_shape=jax.ShapeDtypeStruct((batch, seq, width), BF16),
        compiler_params=_compiler_params(
            ("parallel", "parallel", "arbitrary"),
            pipelined=[((tq, FOX_HEAD_DIM), BF16)] * 2 + [((seq, FOX_HEAD_DIM), BF16)] * 2
                      + [((tq, LANES), F32), ((seq // tq, tq), F32)],
            resident=[((tq, tq), F32)] * 6),
        name="fox_attention",
    )(qkv, qkv, qkv, dec, dec_k).reshape(batch * seq, width)


def _gmlp_layer(h, norm, w_in, v_gain, ws, bs, w_out):
    z = _norm_matmul(h, norm, w_in.astype(BF16), gelu=True)
    return _gmlp_mix(z, v_gain, ws, bs, w_out.astype(BF16), h)


def _swa_layer(h, norm, w_in, sinks, w_out, batch, seq):
    d = h.shape[1]
    qw = SWA_Q_HEADS * SWA_HEAD_DIM
    kvw = SWA_KV_HEADS * SWA_HEAD_DIM

    def duplicate_heads(w):
        w = w.reshape(d, SWA_KV_HEADS, 1, SWA_HEAD_DIM)
        return jnp.broadcast_to(w, (d, SWA_KV_HEADS, 2, SWA_HEAD_DIM)).reshape(d, SWA_KV_HEADS * LANES)

    w = jnp.concatenate([w_in[:, :qw], duplicate_heads(w_in[:, qw:qw + kvw]),
                         duplicate_heads(w_in[:, qw + kvw:])], axis=1).astype(BF16)
    qkv = _norm_matmul(h, norm, w)
    o = _swa_attention(qkv, sinks, batch, seq)
    return _matmul_residual(o, w_out.astype(BF16), h)


def _fox_layer(h, norm, w_in, b_f, w_out, batch, seq):
    d = h.shape[1]
    width = FOX_HEADS * FOX_HEAD_DIM
    w_f = jnp.zeros((d, LANES), BF16).at[:, :FOX_HEADS].set(w_in[:, 3 * width:].astype(BF16))
    qkv, fl = _norm_matmul(h, norm, w_in[:, :3 * width].astype(BF16), w_f)
    dec = _fox_decay(fl, b_f, batch, seq)
    o = _fox_attention(qkv, dec, batch, seq)
    return _matmul_residual(o, w_out.astype(BF16), h)


def kernel(x, l0_ffn1_norm, l0_ffn1_wi, l0_ffn1_wo, l0_mix_norm, l0_mix_win, l0_gmlp_vnorm, l0_gmlp_ws, l0_gmlp_bs, l0_mix_wout, l0_ffn2_norm, l0_ffn2_wi, l0_ffn2_wo, l1_ffn1_norm, l1_ffn1_wi, l1_ffn1_wo, l1_mix_norm, l1_mix_win, l1_swa_sinks, l1_mix_wout, l1_ffn2_norm, l1_ffn2_wi, l1_ffn2_wo, l2_ffn1_norm, l2_ffn1_wi, l2_ffn1_wo, l2_mix_norm, l2_mix_win, l2_fox_bf, l2_mix_wout, l2_ffn2_norm, l2_ffn2_wi, l2_ffn2_wo, l3_ffn1_norm, l3_ffn1_wi, l3_ffn1_wo, l3_mix_norm, l3_mix_win, l3_gmlp_vnorm, l3_gmlp_ws, l3_gmlp_bs, l3_mix_wout, l3_ffn2_norm, l3_ffn2_wi, l3_ffn2_wo, final_norm):
    batch, seq, d = x.shape
    h = x.reshape(batch * seq, d)

    def ffn(h, norm, wi, wo, final_gain=None):
        return _ffn(h, norm, wi.astype(BF16), (0.5 * wo).astype(BF16), final_gain)

    h = ffn(h, l0_ffn1_norm, l0_ffn1_wi, l0_ffn1_wo)
    h = _gmlp_layer(h, l0_mix_norm, l0_mix_win, l0_gmlp_vnorm, l0_gmlp_ws, l0_gmlp_bs, l0_mix_wout)
    h = ffn(h, l0_ffn2_norm, l0_ffn2_wi, l0_ffn2_wo)

    h = ffn(h, l1_ffn1_norm, l1_ffn1_wi, l1_ffn1_wo)
    h = _swa_layer(h, l1_mix_norm, l1_mix_win, l1_swa_sinks, l1_mix_wout, batch, seq)
    h = ffn(h, l1_ffn2_norm, l1_ffn2_wi, l1_ffn2_wo)

    h = ffn(h, l2_ffn1_norm, l2_ffn1_wi, l2_ffn1_wo)
    h = _fox_layer(h, l2_mix_norm, l2_mix_win, l2_fox_bf, l2_mix_wout, batch, seq)
    h = ffn(h, l2_ffn2_norm, l2_ffn2_wi, l2_ffn2_wo)

    h = ffn(h, l3_ffn1_norm, l3_ffn1_wi, l3_ffn1_wo)
    h = _gmlp_layer(h, l3_mix_norm, l3_mix_win, l3_gmlp_vnorm, l3_gmlp_ws, l3_gmlp_bs, l3_mix_wout)
    h = ffn(h, l3_ffn2_norm, l3_ffn2_wi, l3_ffn2_wo, final_norm)
    return h.reshape(batch, seq, d)
```

```python
import functools
import math

import jax
import jax.numpy as jnp
import numpy as np
from jax import lax
from jax.experimental import pallas as pl
from jax.experimental.pallas import tpu as pltpu

F32 = jnp.float32
BF16 = jnp.bfloat16

NORM_EPS = 1e-5
NEG_INF = -1e30
CHUNK = 128
GMLP_GROUPS = 16
SWA_HEAD_DIM = 64
SWA_Q_HEADS = 32
SWA_KV_HEADS = 4
SWA_Q_PER_KV = SWA_Q_HEADS // SWA_KV_HEADS
ROPE_THETA = 500000.0
ROPE_DIM = SWA_HEAD_DIM // 4
FOX_HEAD_DIM = 128
FOX_HEADS = 16

LANES = 128
V7X_VMEM_BYTES = 64 * 1024 * 1024
V7X_VMEM_USABLE_BYTES = 56 * 1024 * 1024


def _nbytes(shape, dtype):
    return math.prod(shape) * jnp.dtype(dtype).itemsize


def _compiler_params(semantics, pipelined, resident):
    need = 2 * sum(_nbytes(s, d) for s, d in pipelined) + sum(_nbytes(s, d) for s, d in resident)
    limit = min(V7X_VMEM_USABLE_BYTES, max(32 * 1024 * 1024, need + need // 4))
    return pltpu.CompilerParams(dimension_semantics=semantics, vmem_limit_bytes=limit)


def _rms(x, gain):
    return x * lax.rsqrt(jnp.mean(x * x, axis=-1, keepdims=True) + NORM_EPS) * gain


def _ffn_kernel(*refs, n_ff_tiles, final_norm):
    if final_norm:
        x_ref, g_ref, wg_ref, wu_ref, wo_ref, gf_ref, o_ref, xn_ref = refs
    else:
        x_ref, g_ref, wg_ref, wu_ref, wo_ref, o_ref, xn_ref = refs
    j = pl.program_id(1)

    @pl.when(j == 0)
    def _():
        x = x_ref[...]
        xn_ref[...] = _rms(x, g_ref[...]).astype(BF16)
        o_ref[...] = x

    xn = xn_ref[...]
    gate = jnp.dot(xn, wg_ref[...], preferred_element_type=F32)
    up = jnp.dot(xn, wu_ref[...], preferred_element_type=F32)
    act = (gate * jax.nn.sigmoid(gate) * up).astype(BF16)
    o_ref[...] += jnp.dot(act, wo_ref[...], preferred_element_type=F32)

    if final_norm:
        @pl.when(j == n_ff_tiles - 1)
        def _():
            o_ref[...] = _rms(o_ref[...], gf_ref[...])


def _ffn(h, gain, wi, wo_half, final_gain=None, *, tm=1024, tf=512):
    wo = wo_half
    t, d = h.shape
    tm = min(tm, t)
    d_ff = wo.shape[0]
    n_ff_tiles = d_ff // tf
    assert t % tm == 0 and d_ff % tf == 0
    in_specs = [
        pl.BlockSpec((tm, d), lambda i, j: (i, 0)),
        pl.BlockSpec((1, d), lambda i, j: (0, 0)),
        pl.BlockSpec((d, tf), lambda i, j: (0, j)),
        pl.BlockSpec((d, tf), lambda i, j: (0, j + n_ff_tiles)),
        pl.BlockSpec((tf, d), lambda i, j: (j, 0)),
    ]
    args = [h, gain.reshape(1, d), wi, wi, wo]
    if final_gain is not None:
        in_specs.append(pl.BlockSpec((1, d), lambda i, j: (0, 0)))
        args.append(final_gain.reshape(1, d))
    return pl.pallas_call(
        functools.partial(_ffn_kernel, n_ff_tiles=n_ff_tiles, final_norm=final_gain is not None),
        grid=(t // tm, n_ff_tiles),
        in_specs=in_specs,
        out_specs=pl.BlockSpec((tm, d), lambda i, j: (i, 0)),
        out_shape=jax.ShapeDtypeStruct((t, d), F32),
        scratch_shapes=[pltpu.VMEM((tm, d), BF16)],
        compiler_params=_compiler_params(
            ("parallel", "arbitrary"),
            pipelined=[((tm, d), F32), ((tm, d), F32), ((d, tf), BF16), ((d, tf), BF16), ((tf, d), BF16)],
            resident=[((tm, d), BF16), ((tm, tf), F32), ((tm, tf), F32), ((tm, tf), F32)]),
        name="ffn",
    )(*args)


def _norm_matmul_kernel(*refs, gelu, side):
    if side:
        x_ref, g_ref, w_ref, w2_ref, o_ref, o2_ref, xn_ref = refs
    else:
        x_ref, g_ref, w_ref, o_ref, xn_ref = refs
    j = pl.program_id(1)

    @pl.when(j == 0)
    def _():
        xn = _rms(x_ref[...], g_ref[...]).astype(BF16)
        xn_ref[...] = xn
        if side:
            o2_ref[...] = jnp.dot(xn, w2_ref[...], preferred_element_type=F32)

    y = jnp.dot(xn_ref[...], w_ref[...], preferred_element_type=F32)
    if gelu:
        y = 0.5 * y * (1.0 + lax.erf(y * np.float32(math.sqrt(0.5))))
    o_ref[...] = y.astype(o_ref.dtype)


def _norm_matmul(h, gain, w, w_side=None, *, gelu=False, tm=1024, tn=1024):
    t, d = h.shape
    tm = min(tm, t)
    n = w.shape[1]
    assert t % tm == 0 and n % tn == 0
    side = w_side is not None
    in_specs = [
        pl.BlockSpec((tm, d), lambda i, j: (i, 0)),
        pl.BlockSpec((1, d), lambda i, j: (0, 0)),
        pl.BlockSpec((d, tn), lambda i, j: (0, j)),
    ]
    args = [h, gain.reshape(1, d), w]
    out_specs = pl.BlockSpec((tm, tn), lambda i, j: (i, j))
    out_shape = jax.ShapeDtypeStruct((t, n), BF16)
    pipelined = [((tm, d), F32), ((d, tn), BF16), ((tm, tn), BF16)]
    if side:
        n2 = w_side.shape[1]
        in_specs.append(pl.BlockSpec((d, n2), lambda i, j: (0, 0)))
        args.append(w_side)
        out_specs = [out_specs, pl.BlockSpec((tm, n2), lambda i, j: (i, 0))]
        out_shape = [out_shape, jax.ShapeDtypeStruct((t, n2), F32)]
        pipelined += [((d, n2), BF16), ((tm, n2), F32)]
    return pl.pallas_call(
        functools.partial(_norm_matmul_kernel, gelu=gelu, side=side),
        grid=(t // tm, n // tn),
        in_specs=in_specs,
        out_specs=out_specs,
        out_shape=out_shape,
        scratch_shapes=[pltpu.VMEM((tm, d), BF16)],
        compiler_params=_compiler_params(
            ("parallel", "arbitrary"), pipelined=pipelined,
            resident=[((tm, d), BF16), ((tm, tn), F32), ((tm, tn), F32)]),
        name="norm_matmul",
    )(*args)


def _matmul_residual_kernel(y_ref, w_ref, h_ref, o_ref):
    o_ref[...] = h_ref[...] + jnp.dot(y_ref[...], w_ref[...], preferred_element_type=F32)


def _matmul_residual(y, w, h, *, tm=512):
    t, k = y.shape
    d = w.shape[1]
    assert t % tm == 0
    return pl.pallas_call(
        _matmul_residual_kernel,
        grid=(t // tm,),
        in_specs=[
            pl.BlockSpec((tm, k), lambda i: (i, 0)),
            pl.BlockSpec((k, d), lambda i: (0, 0)),
            pl.BlockSpec((tm, d), lambda i: (i, 0)),
        ],
        out_specs=pl.BlockSpec((tm, d), lambda i: (i, 0)),
        out_shape=jax.ShapeDtypeStruct((t, d), F32),
        compiler_params=_compiler_params(
            ("parallel",),
            pipelined=[((tm, k), BF16), ((k, d), BF16), ((tm, d), F32), ((tm, d), F32)],
            resident=[((tm, d), F32)]),
        name="matmul_residual",
    )(y, w, h)


def _gmlp_mix_kernel(u_ref, v_ref, vg_ref, ws_ref, bias_ref, wout_ref, h_ref, o_ref, wsm_ref, y_ref,
                     *, n_chunks):
    @pl.when(pl.program_id(0) == 0)
    def _():
        row = lax.broadcasted_iota(jnp.int32, (CHUNK, CHUNK), 0)
        col = lax.broadcasted_iota(jnp.int32, (CHUNK, CHUNK), 1)
        for g in range(GMLP_GROUPS):
            wsm_ref[g] = jnp.where(col <= row, ws_ref[g], 0.0).astype(BF16)

    for c in range(n_chunks):
        rows = slice(c * CHUNK, (c + 1) * CHUNK)
        vn = _rms(v_ref[rows, :].astype(F32), vg_ref[...]).astype(BF16)
        for g in range(GMLP_GROUPS):
            cols = slice(g * LANES, (g + 1) * LANES)
            mixed = jnp.dot(wsm_ref[g], vn[:, cols], preferred_element_type=F32) + bias_ref[:, cols]
            y_ref[rows, cols] = (u_ref[rows, cols].astype(F32) * mixed).astype(BF16)

    o_ref[...] = h_ref[...] + jnp.dot(y_ref[...], wout_ref[...], preferred_element_type=F32)


def _gmlp_mix(z, v_gain, ws, bs, w_out, h, *, tm=256):
    t, d = h.shape
    width = z.shape[1] // 2
    assert width == GMLP_GROUPS * LANES and t % tm == 0 and tm % CHUNK == 0
    n_width_blocks = 1
    bias = jnp.repeat(bs.T, LANES, axis=1)
    return pl.pallas_call(
        functools.partial(_gmlp_mix_kernel, n_chunks=tm // CHUNK),
        grid=(t // tm,),
        in_specs=[
            pl.BlockSpec((tm, width), lambda i: (i, 0)),
            pl.BlockSpec((tm, width), lambda i: (i, n_width_blocks)),
            pl.BlockSpec((1, width), lambda i: (0, 0)),
            pl.BlockSpec((GMLP_GROUPS, CHUNK, CHUNK), lambda i: (0, 0, 0)),
            pl.BlockSpec((CHUNK, width), lambda i: (0, 0)),
            pl.BlockSpec((width, d), lambda i: (0, 0)),
            pl.BlockSpec((tm, d), lambda i: (i, 0)),
        ],
        out_specs=pl.BlockSpec((tm, d), lambda i: (i, 0)),
        out_shape=jax.ShapeDtypeStruct((t, d), F32),
        scratch_shapes=[pltpu.VMEM((GMLP_GROUPS, CHUNK, CHUNK), BF16), pltpu.VMEM((tm, width), BF16)],
        compiler_params=_compiler_params(
            ("arbitrary",),
            pipelined=[((tm, width), BF16), ((tm, width), BF16), ((1, width), F32),
                       ((GMLP_GROUPS, CHUNK, CHUNK), F32), ((CHUNK, width), F32), ((width, d), BF16),
                       ((tm, d), F32), ((tm, d), F32)],
            resident=[((GMLP_GROUPS, CHUNK, CHUNK), BF16), ((tm, width), BF16), ((tm, d), F32)]),
        name="gmlp_mix",
    )(z, z, v_gain.reshape(1, width), ws, bias, w_out, h)


def _rope_tables(seq):
    half = ROPE_DIM // 2
    inv_freq = ROPE_THETA ** (-(jnp.arange(half, dtype=F32) * 2.0 / ROPE_DIM))
    ang = jnp.arange(seq, dtype=F32)[:, None] * inv_freq[None, :]
    cos, sin = jnp.cos(ang), jnp.sin(ang)
    rest = jnp.zeros((seq, SWA_HEAD_DIM - ROPE_DIM), F32)
    cos_h = jnp.concatenate([cos, cos, rest + 1.0], axis=1)
    sin_h = jnp.concatenate([sin, sin, rest], axis=1)
    rot = np.zeros((LANES, LANES), np.float32)
    for i in range(LANES):
        if i % SWA_HEAD_DIM < half:
            rot[i + half, i] = -1.0
        elif i % SWA_HEAD_DIM < ROPE_DIM:
            rot[i - half, i] = 1.0
    return (jnp.concatenate([cos_h, cos_h], axis=1), jnp.concatenate([sin_h, sin_h], axis=1),
            jnp.asarray(rot, BF16))


def _swa_kernel(sinks_ref, q_ref, kc_ref, kp_ref, vc_ref, vp_ref, cos_ref, sin_ref, rot_ref, bias_ref, o_ref):
    n = pl.program_id(1)
    log2e = np.float32(math.log2(math.e))
    cur = pl.ds(pl.multiple_of(n * CHUNK, CHUNK), CHUNK)
    prev = pl.ds(pl.multiple_of(jnp.maximum(n - 1, 0) * CHUNK, CHUNK), CHUNK)
    low_lanes = lax.broadcasted_iota(jnp.int32, (CHUNK, LANES), 1) < SWA_HEAD_DIM
    pairs = SWA_Q_PER_KV // 2
    q_scale = np.float32(SWA_HEAD_DIM ** -0.5 * math.log2(math.e))

    for g in range(SWA_KV_HEADS):
        lanes_g = slice(g * LANES, (g + 1) * LANES)
        pieces = [q_ref[0, :, (g * pairs + pair) * LANES:(g * pairs + pair + 1) * LANES] for pair in range(pairs)]
        pieces += [kc_ref[0, :, lanes_g], kp_ref[0, :, lanes_g]]
        partner = jnp.dot(jnp.concatenate(pieces, axis=0), rot_ref[...], preferred_element_type=F32)

        def rope(idx, rows):
            return (pieces[idx].astype(F32) * cos_ref[rows, :]
                    + partner[idx * CHUNK:(idx + 1) * CHUNK] * sin_ref[rows, :])

        k = jnp.concatenate([rope(pairs + 1, prev), rope(pairs, cur)], axis=0).astype(BF16)
        v = jnp.concatenate([vp_ref[0, :, lanes_g], vc_ref[0, :, lanes_g]], axis=0)
        stacked = []
        for pair in range(pairs):
            q = rope(pair, cur) * q_scale
            stacked += [jnp.where(low_lanes, q, 0.0), jnp.where(low_lanes, 0.0, q)]
        q8 = jnp.concatenate(stacked, axis=0).astype(BF16)
        s = lax.dot_general(q8, k, (((1,), (1,)), ((), ())), preferred_element_type=F32)
        probs, inv_denoms = [], []
        for h in range(SWA_Q_PER_KV):
            rows = slice(h * CHUNK, (h + 1) * CHUNK)
            s_h = s[rows] + bias_ref[0]
            sink = sinks_ref[g * SWA_Q_PER_KV + h] * log2e
            m = jnp.maximum(jnp.max(s_h, axis=1, keepdims=True), sink)
            p = jnp.exp2(s_h - m)
            inv_denoms.append(1.0 / (jnp.sum(p, axis=1, keepdims=True) + jnp.exp2(sink - m)))
            probs.append(p.astype(BF16))
        o8 = jnp.dot(jnp.concatenate(probs, axis=0), v, preferred_element_type=F32)
        for pair in range(pairs):
            lanes_q = slice((g * pairs + pair) * LANES, (g * pairs + pair + 1) * LANES)
            lo = o8[2 * pair * CHUNK:(2 * pair + 1) * CHUNK] * inv_denoms[2 * pair]
            hi = o8[(2 * pair + 1) * CHUNK:(2 * pair + 2) * CHUNK] * inv_denoms[2 * pair + 1]
            o_ref[0, :, lanes_q] = jnp.where(low_lanes, lo, hi).astype(o_ref.dtype)


def _swa_attention(qkv, sinks, batch, seq):
    nb = seq // CHUNK
    qw = SWA_Q_HEADS * SWA_HEAD_DIM
    kw = SWA_KV_HEADS * LANES
    qkv = qkv.reshape(batch, seq, qw + 2 * kw)
    cos, sin, rot = _rope_tables(seq)
    k_blk = qw // kw
    table_spec = pl.BlockSpec((seq, LANES), lambda b, n: (0, 0))
    r = np.arange(CHUNK)[:, None]
    c = np.arange(2 * CHUNK)[None, :]
    band = (c > r) & (c <= r + CHUNK)
    bias = jnp.asarray(np.where(np.stack([band & (c >= CHUNK), band]), 0.0, NEG_INF), F32)
    return pl.pallas_call(
        _swa_kernel,
        grid=(batch, nb),
        in_specs=[
            pl.BlockSpec(memory_space=pltpu.SMEM),
            pl.BlockSpec((1, CHUNK, qw), lambda b, n: (b, n, 0)),
            pl.BlockSpec((1, CHUNK, kw), lambda b, n: (b, n, k_blk)),
            pl.BlockSpec((1, CHUNK, kw), lambda b, n: (b, jnp.maximum(n - 1, 0), k_blk)),
            pl.BlockSpec((1, CHUNK, kw), lambda b, n: (b, n, k_blk + 1)),
            pl.BlockSpec((1, CHUNK, kw), lambda b, n: (b, jnp.maximum(n - 1, 0), k_blk + 1)),
            table_spec, table_spec, pl.BlockSpec((LANES, LANES), lambda b, n: (0, 0)),
            pl.BlockSpec((1, CHUNK, 2 * CHUNK), lambda b, n: (jnp.minimum(n, 1), 0, 0)),
        ],
        out_specs=pl.BlockSpec((1, CHUNK, qw), lambda b, n: (b, n, 0)),
        out_shape=jax.ShapeDtypeStruct((batch, seq, qw), BF16),
        compiler_params=_compiler_params(
            ("parallel", "arbitrary"),
            pipelined=[((CHUNK, qw), BF16)] * 2 + [((CHUNK, kw), BF16)] * 4 + [((seq, LANES), F32)] * 2
                      + [((CHUNK, 2 * CHUNK), F32)],
            resident=[((SWA_Q_PER_KV * CHUNK, 2 * CHUNK), F32)] * 6),
        name="swa_attention",
    )(sinks, qkv, qkv, qkv, qkv, qkv, cos, sin, rot, bias).reshape(batch * seq, qw)


FOX_CUMSUM_CHUNK = 256


def _fox_decay_kernel(fl_ref, bf_ref, o_ref, *, n_chunks):
    cc = FOX_CUMSUM_CHUNK
    row = lax.broadcasted_iota(jnp.int32, (cc, cc), 0)
    col = lax.broadcasted_iota(jnp.int32, (cc, cc), 1)
    lower = jnp.where(col <= row, 1.0, 0.0).astype(BF16)
    carry = jnp.zeros((1, LANES), F32)
    for c in range(n_chunks):
        rows = slice(c * cc, (c + 1) * cc)
        x = jax.nn.log_sigmoid(fl_ref[0, rows, :] + bf_ref[...])
        hi = x.astype(BF16)
        rest = x - hi.astype(F32)
        mid = rest.astype(BF16)
        lo = (rest - mid.astype(F32)).astype(BF16)
        cs = (jnp.dot(lower, hi, preferred_element_type=F32) + jnp.dot(lower, mid, preferred_element_type=F32)
              + jnp.dot(lower, lo, preferred_element_type=F32)) + carry
        o_ref[0, rows, :] = cs
        carry = cs[cc - 1:cc, :]


def _fox_decay(fl, b_f, batch, seq):
    assert seq % FOX_CUMSUM_CHUNK == 0
    bf = jnp.zeros((1, LANES), F32).at[0, :FOX_HEADS].set(b_f)
    return pl.pallas_call(
        functools.partial(_fox_decay_kernel, n_chunks=seq // FOX_CUMSUM_CHUNK),
        grid=(batch,),
        in_specs=[pl.BlockSpec((1, seq, LANES), lambda b: (b, 0, 0)), pl.BlockSpec((1, LANES), lambda b: (0, 0))],
        out_specs=pl.BlockSpec((1, seq, LANES), lambda b: (b, 0, 0)),
        out_shape=jax.ShapeDtypeStruct((batch, seq, LANES), F32),
        compiler_params=_compiler_params(
            ("parallel",), pipelined=[((seq, LANES), F32)] * 2, resident=[((seq, LANES), F32)]),
        name="fox_decay",
    )(fl.reshape(batch, seq, LANES), bf)


def _fox_kernel(q_ref, k_ref, v_ref, dq_ref, dk_ref, o_ref, *, tq, n_tiles):
    head = pl.program_id(1)
    log2e = np.float32(math.log2(math.e))
    scale = np.float32(FOX_HEAD_DIM ** -0.5 * math.log2(math.e))
    head_lane = lax.broadcasted_iota(jnp.int32, (1, LANES), 1) == head
    row = lax.broadcasted_iota(jnp.int32, (tq, tq), 0)
    col = lax.broadcasted_iota(jnp.int32, (tq, tq), 1)
    causal = col <= row

    for i in range(n_tiles):
        rows = slice(i * tq, (i + 1) * tq)
        dq = jnp.sum(jnp.where(head_lane, dq_ref[0, rows, :], 0.0), axis=1, keepdims=True) * log2e
        q = q_ref[0, rows, :]
        m = l = acc = None
        for kt in range(i + 1):
            keys = slice(kt * tq, (kt + 1) * tq)
            s = lax.dot_general(q, k_ref[0, keys, :], (((1,), (1,)), ((), ())), preferred_element_type=F32) * scale
            s = s + dq - dk_ref[0, 0, kt:kt + 1, :] * log2e
            if kt == i:
                s = jnp.where(causal, s, NEG_INF)
            m_tile = jnp.max(s, axis=1, keepdims=True)
            m_new = m_tile if m is None else jnp.maximum(m, m_tile)
            p = jnp.exp2(s - m_new)
            l_tile = jnp.sum(p, axis=1, keepdims=True)
            pv = jnp.dot(p.astype(BF16), v_ref[0, keys, :], preferred_element_type=F32)
            if m is None:
                l, acc = l_tile, pv
            else:
                alpha = jnp.exp2(m - m_new)
                l, acc = alpha * l + l_tile, alpha * acc + pv
            m = m_new
        o_ref[0, rows, :] = (acc / l).astype(o_ref.dtype)


def _fox_attention(qkv, dec, batch, seq, *, tq=512):
    width = FOX_HEADS * FOX_HEAD_DIM
    tq = min(tq, seq)
    assert seq % tq == 0
    qkv = qkv.reshape(batch, seq, 3 * width)
    dec_k = dec[:, :, :FOX_HEADS].transpose(0, 2, 1).reshape(batch, FOX_HEADS, seq // tq, tq)

    def head_spec(first_block):
        return pl.BlockSpec((1, seq, FOX_HEAD_DIM), lambda b, h: (b, 0, first_block + h))

    return pl.pallas_call(
        functools.partial(_fox_kernel, tq=tq, n_tiles=seq // tq),
        grid=(batch, FOX_HEADS),
        in_specs=[
            head_spec(0), head_spec(FOX_HEADS), head_spec(2 * FOX_HEADS),
            pl.BlockSpec((1, seq, LANES), lambda b, h: (b, 0, 0)),
            pl.BlockSpec((1, 1, seq // tq, tq), lambda b, h: (b, h, 0, 0)),
        ],
        out_specs=head_spec(0),
        out_shape=jax.ShapeDtypeStruct((batch, seq, width), BF16),
        compiler_params=_compiler_params(
            ("parallel", "parallel"),
            pipelined=[((seq, FOX_HEAD_DIM), BF16)] * 4 + [((seq, LANES), F32), ((seq // tq, tq), F32)],
            resident=[((tq, tq), F32)] * 12),
        name="fox_attention",
    )(qkv, qkv, qkv, dec, dec_k).reshape(batch * seq, width)


def _gmlp_layer(h, norm, w_in, v_gain, ws, bs, w_out):
    z = _norm_matmul(h, norm, w_in.astype(BF16), gelu=True)
    return _gmlp_mix(z, v_gain, ws, bs, w_out.astype(BF16), h)


def _swa_layer(h, norm, w_in, sinks, w_out, batch, seq):
    d = h.shape[1]
    qw = SWA_Q_HEADS * SWA_HEAD_DIM
    kvw = SWA_KV_HEADS * SWA_HEAD_DIM

    def duplicate_heads(w):
        w = w.reshape(d, SWA_KV_HEADS, 1, SWA_HEAD_DIM)
        return jnp.broadcast_to(w, (d, SWA_KV_HEADS, 2, SWA_HEAD_DIM)).reshape(d, SWA_KV_HEADS * LANES)

    w = jnp.concatenate([w_in[:, :qw], duplicate_heads(w_in[:, qw:qw + kvw]),
                         duplicate_heads(w_in[:, qw + kvw:])], axis=1).astype(BF16)
    qkv = _norm_matmul(h, norm, w)
    o = _swa_attention(qkv, sinks, batch, seq)
    return _matmul_residual(o, w_out.astype(BF16), h)


def _fox_layer(h, norm, w_in, b_f, w_out, batch, seq):
    d = h.shape[1]
    width = FOX_HEADS * FOX_HEAD_DIM
    w_f = jnp.zeros((d, LANES), BF16).at[:, :FOX_HEADS].set(w_in[:, 3 * width:].astype(BF16))
    qkv, fl = _norm_matmul(h, norm, w_in[:, :3 * width].astype(BF16), w_f)
    dec = _fox_decay(fl, b_f, batch, seq)
    o = _fox_attention(qkv, dec, batch, seq)
    return _matmul_residual(o, w_out.astype(BF16), h)


def kernel(x, l0_ffn1_norm, l0_ffn1_wi, l0_ffn1_wo, l0_mix_norm, l0_mix_win, l0_gmlp_vnorm, l0_gmlp_ws, l0_gmlp_bs, l0_mix_wout, l0_ffn2_norm, l0_ffn2_wi, l0_ffn2_wo, l1_ffn1_norm, l1_ffn1_wi, l1_ffn1_wo, l1_mix_norm, l1_mix_win, l1_swa_sinks, l1_mix_wout, l1_ffn2_norm, l1_ffn2_wi, l1_ffn2_wo, l2_ffn1_norm, l2_ffn1_wi, l2_ffn1_wo, l2_mix_norm, l2_mix_win, l2_fox_bf, l2_mix_wout, l2_ffn2_norm, l2_ffn2_wi, l2_ffn2_wo, l3_ffn1_norm, l3_ffn1_wi, l3_ffn1_wo, l3_mix_norm, l3_mix_win, l3_gmlp_vnorm, l3_gmlp_ws, l3_gmlp_bs, l3_mix_wout, l3_ffn2_norm, l3_ffn2_wi, l3_ffn2_wo, final_norm):
    batch, seq, d = x.shape
    h = x.reshape(batch * seq, d)

    def ffn(h, norm, wi, wo, final_gain=None):
        return _ffn(h, norm, wi.astype(BF16), (0.5 * wo).astype(BF16), final_gain)

    h = ffn(h, l0_ffn1_norm, l0_ffn1_wi, l0_ffn1_wo)
    h = _gmlp_layer(h, l0_mix_norm, l0_mix_win, l0_gmlp_vnorm, l0_gmlp_ws, l0_gmlp_bs, l0_mix_wout)
    h = ffn(h, l0_ffn2_norm, l0_ffn2_wi, l0_ffn2_wo)

    h = ffn(h, l1_ffn1_norm, l1_ffn1_wi, l1_ffn1_wo)
    h = _swa_layer(h, l1_mix_norm, l1_mix_win, l1_swa_sinks, l1_mix_wout, batch, seq)
    h = ffn(h, l1_ffn2_norm, l1_ffn2_wi, l1_ffn2_wo)

    h = ffn(h, l2_ffn1_norm, l2_ffn1_wi, l2_ffn1_wo)
    h = _fox_layer(h, l2_mix_norm, l2_mix_win, l2_fox_bf, l2_mix_wout, batch, seq)
    h = ffn(h, l2_ffn2_norm, l2_ffn2_wi, l2_ffn2_wo)

    h = ffn(h, l3_ffn1_norm, l3_ffn1_wi, l3_ffn1_wo)
    h = _gmlp_layer(h, l3_mix_norm, l3_mix_win, l3_gmlp_vnorm, l3_gmlp_ws, l3_gmlp_bs, l3_mix_wout)
    h = ffn(h, l3_ffn2_norm, l3_ffn2_wi, l3_ffn2_wo, final_norm)
    return h.reshape(batch, seq, d)
```

```python
import functools
import math

import jax
import jax.numpy as jnp
import numpy as np
from jax import lax
from jax.experimental import pallas as pl
from jax.experimental.pallas import tpu as pltpu

F32 = jnp.float32
BF16 = jnp.bfloat16

NORM_EPS = 1e-5
NEG_INF = -1e30
CHUNK = 128
GMLP_GROUPS = 16
SWA_HEAD_DIM = 64
SWA_Q_HEADS = 32
SWA_KV_HEADS = 4
SWA_Q_PER_KV = SWA_Q_HEADS // SWA_KV_HEADS
ROPE_THETA = 500000.0
ROPE_DIM = SWA_HEAD_DIM // 4
FOX_HEAD_DIM = 128
FOX_HEADS = 16

LANES = 128
V7X_VMEM_BYTES = 64 * 1024 * 1024
V7X_VMEM_USABLE_BYTES = 60 * 1024 * 1024


def _nbytes(shape, dtype):
    return math.prod(shape) * jnp.dtype(dtype).itemsize


def _compiler_params(semantics, pipelined, resident):
    need = 2 * sum(_nbytes(s, d) for s, d in pipelined) + sum(_nbytes(s, d) for s, d in resident)
    limit = min(V7X_VMEM_USABLE_BYTES, max(32 * 1024 * 1024, need + need // 4))
    return pltpu.CompilerParams(dimension_semantics=semantics, vmem_limit_bytes=limit)


def _rms(x, gain):
    return x * lax.rsqrt(jnp.mean(x * x, axis=-1, keepdims=True) + NORM_EPS) * gain


def _ffn_kernel(*refs, n_ff_tiles, final_norm, cast_next):
    refs = list(refs)
    x_ref, g_ref, wg_ref, wu_ref, wo_ref = refs[:5]
    del refs[:5]
    gf_ref = refs.pop(0) if final_norm else None
    if cast_next:
        wi_next_ref, wo_next_ref = refs[:2]
        del refs[:2]
    o_ref = refs.pop(0)
    if cast_next:
        wi_next_bf16_ref, wo_next_bf16_ref = refs[:2]
        del refs[:2]
    (xn_ref,) = refs
    j = pl.program_id(1)

    @pl.when(j == 0)
    def _():
        x = x_ref[...]
        xn_ref[...] = _rms(x, g_ref[...]).astype(BF16)
        o_ref[...] = x

    xn = xn_ref[...]
    gate = jnp.dot(xn, wg_ref[...], preferred_element_type=F32)
    up = jnp.dot(xn, wu_ref[...], preferred_element_type=F32)
    act = (gate * jax.nn.sigmoid(gate) * up).astype(BF16)
    o_ref[...] += jnp.dot(act, wo_ref[...], preferred_element_type=F32)

    if cast_next:
        wi_next_bf16_ref[...] = wi_next_ref[...].astype(BF16)
        wo_next_bf16_ref[...] = (0.5 * wo_next_ref[...]).astype(BF16)

    if final_norm:
        @pl.when(j == n_ff_tiles - 1)
        def _():
            o_ref[...] = _rms(o_ref[...], gf_ref[...])


def _ffn(h, gain, wi, wo_half, final_gain=None, next_weights=None, *, tm=1024, tf=512):
    wo = wo_half
    t, d = h.shape
    tm = min(tm, t)
    d_ff = wo.shape[0]
    n_ff_tiles = d_ff // tf
    n_row_tiles = t // tm
    assert t % tm == 0 and d_ff % tf == 0
    in_specs = [
        pl.BlockSpec((tm, d), lambda i, j: (i, 0)),
        pl.BlockSpec((1, d), lambda i, j: (0, 0)),
        pl.BlockSpec((d, tf), lambda i, j: (0, j)),
        pl.BlockSpec((d, tf), lambda i, j: (0, j + n_ff_tiles)),
        pl.BlockSpec((tf, d), lambda i, j: (j, 0)),
    ]
    args = [h, gain.reshape(1, d), wi, wi, wo]
    out_specs = [pl.BlockSpec((tm, d), lambda i, j: (i, 0))]
    out_shape = [jax.ShapeDtypeStruct((t, d), F32)]
    pipelined = [((tm, d), F32), ((tm, d), F32), ((d, tf), BF16), ((d, tf), BF16), ((tf, d), BF16)]
    if final_gain is not None:
        in_specs.append(pl.BlockSpec((1, d), lambda i, j: (0, 0)))
        args.append(final_gain.reshape(1, d))
    if next_weights is not None:
        wi_next, wo_next = next_weights
        assert wi_next.shape == wi.shape and wo_next.shape == wo.shape
        assert d % n_row_tiles == 0 and wi.shape[1] % n_ff_tiles == 0
        wi_block = (d // n_row_tiles, wi.shape[1] // n_ff_tiles)
        wo_block = (d_ff // n_ff_tiles, d // n_row_tiles)
        wi_spec = pl.BlockSpec(wi_block, lambda i, j: (i, j))
        wo_spec = pl.BlockSpec(wo_block, lambda i, j: (j, i))
        in_specs += [wi_spec, wo_spec]
        args += [wi_next, wo_next]
        out_specs += [wi_spec, wo_spec]
        out_shape += [jax.ShapeDtypeStruct(wi.shape, BF16), jax.ShapeDtypeStruct(wo.shape, BF16)]
        pipelined += [(wi_block, F32), (wo_block, F32), (wi_block, BF16), (wo_block, BF16)]
    outs = pl.pallas_call(
        functools.partial(_ffn_kernel, n_ff_tiles=n_ff_tiles, final_norm=final_gain is not None,
                          cast_next=next_weights is not None),
        grid=(n_row_tiles, n_ff_tiles),
        in_specs=in_specs,
        out_specs=out_specs,
        out_shape=out_shape,
        scratch_shapes=[pltpu.VMEM((tm, d), BF16)],
        compiler_params=_compiler_params(
            ("parallel", "arbitrary"), pipelined=pipelined,
            resident=[((tm, d), BF16), ((tm, tf), F32), ((tm, tf), F32), ((tm, tf), F32)]),
        name="ffn",
    )(*args)
    return outs[0] if next_weights is None else tuple(outs)


def _norm_matmul_kernel(*refs, gelu, side):
    if side:
        x_ref, g_ref, w_ref, w2_ref, o_ref, o2_ref, xn_ref = refs
    else:
        x_ref, g_ref, w_ref, o_ref, xn_ref = refs
    j = pl.program_id(1)

    @pl.when(j == 0)
    def _():
        xn = _rms(x_ref[...], g_ref[...]).astype(BF16)
        xn_ref[...] = xn
        if side:
            o2_ref[...] = jnp.dot(xn, w2_ref[...], preferred_element_type=F32)

    y = jnp.dot(xn_ref[...], w_ref[...], preferred_element_type=F32)
    if gelu:
        y = 0.5 * y * (1.0 + lax.erf(y * np.float32(math.sqrt(0.5))))
    o_ref[...] = y.astype(o_ref.dtype)


def _norm_matmul(h, gain, w, w_side=None, *, gelu=False, tm=1024, tn=1024):
    t, d = h.shape
    tm = min(tm, t)
    n = w.shape[1]
    assert t % tm == 0 and n % tn == 0
    side = w_side is not None
    in_specs = [
        pl.BlockSpec((tm, d), lambda i, j: (i, 0)),
        pl.BlockSpec((1, d), lambda i, j: (0, 0)),
        pl.BlockSpec((d, tn), lambda i, j: (0, j)),
    ]
    args = [h, gain.reshape(1, d), w]
    out_specs = pl.BlockSpec((tm, tn), lambda i, j: (i, j))
    out_shape = jax.ShapeDtypeStruct((t, n), BF16)
    pipelined = [((tm, d), F32), ((d, tn), BF16), ((tm, tn), BF16)]
    if side:
        n2 = w_side.shape[1]
        in_specs.append(pl.BlockSpec((d, n2), lambda i, j: (0, 0)))
        args.append(w_side)
        out_specs = [out_specs, pl.BlockSpec((tm, n2), lambda i, j: (i, 0))]
        out_shape = [out_shape, jax.ShapeDtypeStruct((t, n2), F32)]
        pipelined += [((d, n2), BF16), ((tm, n2), F32)]
    return pl.pallas_call(
        functools.partial(_norm_matmul_kernel, gelu=gelu, side=side),
        grid=(t // tm, n // tn),
        in_specs=in_specs,
        out_specs=out_specs,
        out_shape=out_shape,
        scratch_shapes=[pltpu.VMEM((tm, d), BF16)],
        compiler_params=_compiler_params(
            ("parallel", "arbitrary"), pipelined=pipelined,
            resident=[((tm, d), BF16), ((tm, tn), F32), ((tm, tn), F32)]),
        name="norm_matmul",
    )(*args)


def _matmul_residual_kernel(y_ref, w_ref, h_ref, o_ref):
    o_ref[...] = h_ref[...] + jnp.dot(y_ref[...], w_ref[...], preferred_element_type=F32)


def _matmul_residual(y, w, h, *, tm=512):
    t, k = y.shape
    d = w.shape[1]
    assert t % tm == 0
    return pl.pallas_call(
        _matmul_residual_kernel,
        grid=(t // tm,),
        in_specs=[
            pl.BlockSpec((tm, k), lambda i: (i, 0)),
            pl.BlockSpec((k, d), lambda i: (0, 0)),
            pl.BlockSpec((tm, d), lambda i: (i, 0)),
        ],
        out_specs=pl.BlockSpec((tm, d), lambda i: (i, 0)),
        out_shape=jax.ShapeDtypeStruct((t, d), F32),
        compiler_params=_compiler_params(
            ("parallel",),
            pipelined=[((tm, k), BF16), ((k, d), BF16), ((tm, d), F32), ((tm, d), F32)],
            resident=[((tm, d), F32)]),
        name="matmul_residual",
    )(y, w, h)


def _gmlp_mix_kernel(u_ref, v_ref, vg_ref, ws_ref, bias_ref, wout_ref, h_ref, o_ref, wsm_ref, y_ref,
                     *, n_chunks):
    @pl.when(pl.program_id(0) == 0)
    def _():
        row = lax.broadcasted_iota(jnp.int32, (CHUNK, CHUNK), 0)
        col = lax.broadcasted_iota(jnp.int32, (CHUNK, CHUNK), 1)
        for g in range(GMLP_GROUPS):
            wsm_ref[g] = jnp.where(col <= row, ws_ref[g], 0.0).astype(BF16)

    for c in range(n_chunks):
        rows = slice(c * CHUNK, (c + 1) * CHUNK)
        vn = _rms(v_ref[rows, :].astype(F32), vg_ref[...]).astype(BF16)
        for g in range(GMLP_GROUPS):
            cols = slice(g * LANES, (g + 1) * LANES)
            mixed = jnp.dot(wsm_ref[g], vn[:, cols], preferred_element_type=F32) + bias_ref[:, cols]
            y_ref[rows, cols] = (u_ref[rows, cols].astype(F32) * mixed).astype(BF16)

    o_ref[...] = h_ref[...] + jnp.dot(y_ref[...], wout_ref[...], preferred_element_type=F32)


def _gmlp_mix(z, v_gain, ws, bs, w_out, h, *, tm=256):
    t, d = h.shape
    width = z.shape[1] // 2
    assert width == GMLP_GROUPS * LANES and t % tm == 0 and tm % CHUNK == 0
    n_width_blocks = 1
    bias = jnp.repeat(bs.T, LANES, axis=1)
    return pl.pallas_call(
        functools.partial(_gmlp_mix_kernel, n_chunks=tm // CHUNK),
        grid=(t // tm,),
        in_specs=[
            pl.BlockSpec((tm, width), lambda i: (i, 0)),
            pl.BlockSpec((tm, width), lambda i: (i, n_width_blocks)),
            pl.BlockSpec((1, width), lambda i: (0, 0)),
            pl.BlockSpec((GMLP_GROUPS, CHUNK, CHUNK), lambda i: (0, 0, 0)),
            pl.BlockSpec((CHUNK, width), lambda i: (0, 0)),
            pl.BlockSpec((width, d), lambda i: (0, 0)),
            pl.BlockSpec((tm, d), lambda i: (i, 0)),
        ],
        out_specs=pl.BlockSpec((tm, d), lambda i: (i, 0)),
        out_shape=jax.ShapeDtypeStruct((t, d), F32),
        scratch_shapes=[pltpu.VMEM((GMLP_GROUPS, CHUNK, CHUNK), BF16), pltpu.VMEM((tm, width), BF16)],
        compiler_params=_compiler_params(
            ("arbitrary",),
            pipelined=[((tm, width), BF16), ((tm, width), BF16), ((1, width), F32),
                       ((GMLP_GROUPS, CHUNK, CHUNK), F32), ((CHUNK, width), F32), ((width, d), BF16),
                       ((tm, d), F32), ((tm, d), F32)],
            resident=[((GMLP_GROUPS, CHUNK, CHUNK), BF16), ((tm, width), BF16), ((tm, d), F32)]),
        name="gmlp_mix",
    )(z, z, v_gain.reshape(1, width), ws, bias, w_out, h)


def _rope_tables(seq):
    half = ROPE_DIM // 2
    inv_freq = ROPE_THETA ** (-(jnp.arange(half, dtype=F32) * 2.0 / ROPE_DIM))
    ang = jnp.arange(seq, dtype=F32)[:, None] * inv_freq[None, :]
    cos, sin = jnp.cos(ang), jnp.sin(ang)
    rest = jnp.zeros((seq, SWA_HEAD_DIM - ROPE_DIM), F32)
    cos_h = jnp.concatenate([cos, cos, rest + 1.0], axis=1)
    sin_h = jnp.concatenate([sin, sin, rest], axis=1)
    rot = np.zeros((LANES, LANES), np.float32)
    for i in range(LANES):
        if i % SWA_HEAD_DIM < half:
            rot[i + half, i] = -1.0
        elif i % SWA_HEAD_DIM < ROPE_DIM:
            rot[i - half, i] = 1.0
    return (jnp.concatenate([cos_h, cos_h], axis=1), jnp.concatenate([sin_h, sin_h], axis=1),
            jnp.asarray(rot, BF16))


def _swa_kernel(sinks_ref, q_ref, kc_ref, kp_ref, vc_ref, vp_ref, cos_ref, sin_ref, rot_ref, bias_ref, o_ref):
    n = pl.program_id(1)
    log2e = np.float32(math.log2(math.e))
    cur = pl.ds(pl.multiple_of(n * CHUNK, CHUNK), CHUNK)
    prev = pl.ds(pl.multiple_of(jnp.maximum(n - 1, 0) * CHUNK, CHUNK), CHUNK)
    low_lanes = lax.broadcasted_iota(jnp.int32, (CHUNK, LANES), 1) < SWA_HEAD_DIM
    pairs = SWA_Q_PER_KV // 2
    q_scale = np.float32(SWA_HEAD_DIM ** -0.5 * math.log2(math.e))

    for g in range(SWA_KV_HEADS):
        lanes_g = slice(g * LANES, (g + 1) * LANES)
        pieces = [q_ref[0, :, (g * pairs + pair) * LANES:(g * pairs + pair + 1) * LANES] for pair in range(pairs)]
        pieces += [kc_ref[0, :, lanes_g], kp_ref[0, :, lanes_g]]
        partner = jnp.dot(jnp.concatenate(pieces, axis=0), rot_ref[...], preferred_element_type=F32)

        def rope(idx, rows):
            return (pieces[idx].astype(F32) * cos_ref[rows, :]
                    + partner[idx * CHUNK:(idx + 1) * CHUNK] * sin_ref[rows, :])

        k = jnp.concatenate([rope(pairs + 1, prev), rope(pairs, cur)], axis=0).astype(BF16)
        v = jnp.concatenate([vp_ref[0, :, lanes_g], vc_ref[0, :, lanes_g]], axis=0)
        stacked = []
        for pair in range(pairs):
            q = rope(pair, cur) * q_scale
            stacked += [jnp.where(low_lanes, q, 0.0), jnp.where(low_lanes, 0.0, q)]
        q8 = jnp.concatenate(stacked, axis=0).astype(BF16)
        s = lax.dot_general(q8, k, (((1,), (1,)), ((), ())), preferred_element_type=F32)
        probs, inv_denoms = [], []
        for h in range(SWA_Q_PER_KV):
            rows = slice(h * CHUNK, (h + 1) * CHUNK)
            s_h = s[rows] + bias_ref[0]
            sink = sinks_ref[g * SWA_Q_PER_KV + h] * log2e
            m = jnp.maximum(jnp.max(s_h, axis=1, keepdims=True), sink)
            p = jnp.exp2(s_h - m)
            inv_denoms.append(1.0 / (jnp.sum(p, axis=1, keepdims=True) + jnp.exp2(sink - m)))
            probs.append(p.astype(BF16))
        o8 = jnp.dot(jnp.concatenate(probs, axis=0), v, preferred_element_type=F32)
        for pair in range(pairs):
            lanes_q = slice((g * pairs + pair) * LANES, (g * pairs + pair + 1) * LANES)
            lo = o8[2 * pair * CHUNK:(2 * pair + 1) * CHUNK] * inv_denoms[2 * pair]
            hi = o8[(2 * pair + 1) * CHUNK:(2 * pair + 2) * CHUNK] * inv_denoms[2 * pair + 1]
            o_ref[0, :, lanes_q] = jnp.where(low_lanes, lo, hi).astype(o_ref.dtype)


def _swa_attention(qkv, sinks, batch, seq):
    nb = seq // CHUNK
    qw = SWA_Q_HEADS * SWA_HEAD_DIM
    kw = SWA_KV_HEADS * LANES
    qkv = qkv.reshape(batch, seq, qw + 2 * kw)
    cos, sin, rot = _rope_tables(seq)
    k_blk = qw // kw
    table_spec = pl.BlockSpec((seq, LANES), lambda b, n: (0, 0))
    r = np.arange(CHUNK)[:, None]
    c = np.arange(2 * CHUNK)[None, :]
    band = (c > r) & (c <= r + CHUNK)
    bias = jnp.asarray(np.where(np.stack([band & (c >= CHUNK), band]), 0.0, NEG_INF), F32)
    return pl.pallas_call(
        _swa_kernel,
        grid=(batch, nb),
        in_specs=[
            pl.BlockSpec(memory_space=pltpu.SMEM),
            pl.BlockSpec((1, CHUNK, qw), lambda b, n: (b, n, 0)),
            pl.BlockSpec((1, CHUNK, kw), lambda b, n: (b, n, k_blk)),
            pl.BlockSpec((1, CHUNK, kw), lambda b, n: (b, jnp.maximum(n - 1, 0), k_blk)),
            pl.BlockSpec((1, CHUNK, kw), lambda b, n: (b, n, k_blk + 1)),
            pl.BlockSpec((1, CHUNK, kw), lambda b, n: (b, jnp.maximum(n - 1, 0), k_blk + 1)),
            table_spec, table_spec, pl.BlockSpec((LANES, LANES), lambda b, n: (0, 0)),
            pl.BlockSpec((1, CHUNK, 2 * CHUNK), lambda b, n: (jnp.minimum(n, 1), 0, 0)),
        ],
        out_specs=pl.BlockSpec((1, CHUNK, qw), lambda b, n: (b, n, 0)),
        out_shape=jax.ShapeDtypeStruct((batch, seq, qw), BF16),
        compiler_params=_compiler_params(
            ("parallel", "arbitrary"),
            pipelined=[((CHUNK, qw), BF16)] * 2 + [((CHUNK, kw), BF16)] * 4 + [((seq, LANES), F32)] * 2
                      + [((CHUNK, 2 * CHUNK), F32)],
            resident=[((SWA_Q_PER_KV * CHUNK, 2 * CHUNK), F32)] * 6),
        name="swa_attention",
    )(sinks, qkv, qkv, qkv, qkv, qkv, cos, sin, rot, bias).reshape(batch * seq, qw)


FOX_CUMSUM_CHUNK = 256


def _fox_decay_kernel(fl_ref, bf_ref, o_ref, *, n_chunks):
    cc = FOX_CUMSUM_CHUNK
    row = lax.broadcasted_iota(jnp.int32, (cc, cc), 0)
    col = lax.broadcasted_iota(jnp.int32, (cc, cc), 1)
    lower = jnp.where(col <= row, 1.0, 0.0).astype(BF16)
    carry = jnp.zeros((1, LANES), F32)
    for c in range(n_chunks):
        rows = slice(c * cc, (c + 1) * cc)
        x = jax.nn.log_sigmoid(fl_ref[0, rows, :] + bf_ref[...])
        hi = x.astype(BF16)
        rest = x - hi.astype(F32)
        mid = rest.astype(BF16)
        lo = (rest - mid.astype(F32)).astype(BF16)
        cs = (jnp.dot(lower, hi, preferred_element_type=F32) + jnp.dot(lower, mid, preferred_element_type=F32)
              + jnp.dot(lower, lo, preferred_element_type=F32)) + carry
        o_ref[0, rows, :] = cs
        carry = cs[cc - 1:cc, :]


def _fox_decay(fl, b_f, batch, seq):
    assert seq % FOX_CUMSUM_CHUNK == 0
    bf = jnp.zeros((1, LANES), F32).at[0, :FOX_HEADS].set(b_f)
    return pl.pallas_call(
        functools.partial(_fox_decay_kernel, n_chunks=seq // FOX_CUMSUM_CHUNK),
        grid=(batch,),
        in_specs=[pl.BlockSpec((1, seq, LANES), lambda b: (b, 0, 0)), pl.BlockSpec((1, LANES), lambda b: (0, 0))],
        out_specs=pl.BlockSpec((1, seq, LANES), lambda b: (b, 0, 0)),
        out_shape=jax.ShapeDtypeStruct((batch, seq, LANES), F32),
        compiler_params=_compiler_params(
            ("parallel",), pipelined=[((seq, LANES), F32)] * 2, resident=[((seq, LANES), F32)]),
        name="fox_decay",
    )(fl.reshape(batch, seq, LANES), bf)


def _fox_kernel(q_ref, k_ref, v_ref, dq_ref, dk_ref, o_ref, *, tq, n_tiles):
    head = pl.program_id(1)
    log2e = np.float32(math.log2(math.e))
    scale = np.float32(FOX_HEAD_DIM ** -0.5 * math.log2(math.e))
    head_lane = lax.broadcasted_iota(jnp.int32, (1, LANES), 1) == head
    row = lax.broadcasted_iota(jnp.int32, (tq, tq), 0)
    col = lax.broadcasted_iota(jnp.int32, (tq, tq), 1)
    causal = col <= row

    for i in range(n_tiles):
        rows = slice(i * tq, (i + 1) * tq)
        dq = jnp.sum(jnp.where(head_lane, dq_ref[0, rows, :], 0.0), axis=1, keepdims=True) * log2e
        q = q_ref[0, rows, :]
        m = l = acc = None
        for kt in range(i + 1):
            keys = slice(kt * tq, (kt + 1) * tq)
            s = lax.dot_general(q, k_ref[0, keys, :], (((1,), (1,)), ((), ())), preferred_element_type=F32) * scale
            s = s + dq - dk_ref[0, 0, kt:kt + 1, :] * log2e
            if kt == i:
                s = jnp.where(causal, s, NEG_INF)
            m_tile = jnp.max(s, axis=1, keepdims=True)
            m_new = m_tile if m is None else jnp.maximum(m, m_tile)
            p = jnp.exp2(s - m_new)
            l_tile = jnp.sum(p, axis=1, keepdims=True)
            pv = jnp.dot(p.astype(BF16), v_ref[0, keys, :], preferred_element_type=F32)
            if m is None:
                l, acc = l_tile, pv
            else:
                alpha = jnp.exp2(m - m_new)
                l, acc = alpha * l + l_tile, alpha * acc + pv
            m = m_new
        o_ref[0, rows, :] = (acc / l).astype(o_ref.dtype)


def _fox_attention(qkv, dec, batch, seq, *, tq=512):
    width = FOX_HEADS * FOX_HEAD_DIM
    tq = min(tq, seq)
    assert seq % tq == 0
    qkv = qkv.reshape(batch, seq, 3 * width)
    dec_k = dec[:, :, :FOX_HEADS].transpose(0, 2, 1).reshape(batch, FOX_HEADS, seq // tq, tq)

    def head_spec(first_block):
        return pl.BlockSpec((1, seq, FOX_HEAD_DIM), lambda b, h: (b, 0, first_block + h))

    return pl.pallas_call(
        functools.partial(_fox_kernel, tq=tq, n_tiles=seq // tq),
        grid=(batch, FOX_HEADS),
        in_specs=[
            head_spec(0), head_spec(FOX_HEADS), head_spec(2 * FOX_HEADS),
            pl.BlockSpec((1, seq, LANES), lambda b, h: (b, 0, 0)),
            pl.BlockSpec((1, 1, seq // tq, tq), lambda b, h: (b, h, 0, 0)),
        ],
        out_specs=head_spec(0),
        out_shape=jax.ShapeDtypeStruct((batch, seq, width), BF16),
        compiler_params=_compiler_params(
            ("parallel", "parallel"),
            pipelined=[((seq, FOX_HEAD_DIM), BF16)] * 4 + [((seq, LANES), F32), ((seq // tq, tq), F32)],
            resident=[((tq, tq), F32)] * 12),
        name="fox_attention",
    )(qkv, qkv, qkv, dec, dec_k).reshape(batch * seq, width)


def _gmlp_layer(h, norm, w_in, v_gain, ws, bs, w_out):
    z = _norm_matmul(h, norm, w_in.astype(BF16), gelu=True)
    return _gmlp_mix(z, v_gain, ws, bs, w_out.astype(BF16), h)


def _swa_layer(h, norm, w_in, sinks, w_out, batch, seq):
    d = h.shape[1]
    qw = SWA_Q_HEADS * SWA_HEAD_DIM
    kvw = SWA_KV_HEADS * SWA_HEAD_DIM

    def duplicate_heads(w):
        w = w.reshape(d, SWA_KV_HEADS, 1, SWA_HEAD_DIM)
        return jnp.broadcast_to(w, (d, SWA_KV_HEADS, 2, SWA_HEAD_DIM)).reshape(d, SWA_KV_HEADS * LANES)

    w = jnp.concatenate([w_in[:, :qw], duplicate_heads(w_in[:, qw:qw + kvw]),
                         duplicate_heads(w_in[:, qw + kvw:])], axis=1).astype(BF16)
    qkv = _norm_matmul(h, norm, w)
    o = _swa_attention(qkv, sinks, batch, seq)
    return _matmul_residual(o, w_out.astype(BF16), h)


def _fox_layer(h, norm, w_in, b_f, w_out, batch, seq):
    d = h.shape[1]
    width = FOX_HEADS * FOX_HEAD_DIM
    w_f = jnp.zeros((d, LANES), BF16).at[:, :FOX_HEADS].set(w_in[:, 3 * width:].astype(BF16))
    qkv, fl = _norm_matmul(h, norm, w_in[:, :3 * width].astype(BF16), w_f)
    dec = _fox_decay(fl, b_f, batch, seq)
    o = _fox_attention(qkv, dec, batch, seq)
    return _matmul_residual(o, w_out.astype(BF16), h)


def kernel(x, l0_ffn1_norm, l0_ffn1_wi, l0_ffn1_wo, l0_mix_norm, l0_mix_win, l0_gmlp_vnorm, l0_gmlp_ws, l0_gmlp_bs, l0_mix_wout, l0_ffn2_norm, l0_ffn2_wi, l0_ffn2_wo, l1_ffn1_norm, l1_ffn1_wi, l1_ffn1_wo, l1_mix_norm, l1_mix_win, l1_swa_sinks, l1_mix_wout, l1_ffn2_norm, l1_ffn2_wi, l1_ffn2_wo, l2_ffn1_norm, l2_ffn1_wi, l2_ffn1_wo, l2_mix_norm, l2_mix_win, l2_fox_bf, l2_mix_wout, l2_ffn2_norm, l2_ffn2_wi, l2_ffn2_wo, l3_ffn1_norm, l3_ffn1_wi, l3_ffn1_wo, l3_mix_norm, l3_mix_win, l3_gmlp_vnorm, l3_gmlp_ws, l3_gmlp_bs, l3_mix_wout, l3_ffn2_norm, l3_ffn2_wi, l3_ffn2_wo, final_norm):
    batch, seq, d = x.shape
    h = x.reshape(batch * seq, d)

    ffn_weights = [(l0_ffn1_wi, l0_ffn1_wo), (l0_ffn2_wi, l0_ffn2_wo), (l1_ffn1_wi, l1_ffn1_wo),
                   (l1_ffn2_wi, l1_ffn2_wo), (l2_ffn1_wi, l2_ffn1_wo), (l2_ffn2_wi, l2_ffn2_wo),
                   (l3_ffn1_wi, l3_ffn1_wo), (l3_ffn2_wi, l3_ffn2_wo)]
    cast = [(ffn_weights[0][0].astype(BF16), (0.5 * ffn_weights[0][1]).astype(BF16))]

    def ffn(h, norm, final_gain=None):
        index = len(cast) - 1
        wi, wo_half = cast[index]
        if index + 1 == len(ffn_weights):
            return _ffn(h, norm, wi, wo_half, final_gain)
        h, wi_next, wo_next = _ffn(h, norm, wi, wo_half, final_gain, ffn_weights[index + 1])
        cast.append((wi_next, wo_next))
        return h

    h = ffn(h, l0_ffn1_norm)
    h = _gmlp_layer(h, l0_mix_norm, l0_mix_win, l0_gmlp_vnorm, l0_gmlp_ws, l0_gmlp_bs, l0_mix_wout)
    h = ffn(h, l0_ffn2_norm)

    h = ffn(h, l1_ffn1_norm)
    h = _swa_layer(h, l1_mix_norm, l1_mix_win, l1_swa_sinks, l1_mix_wout, batch, seq)
    h = ffn(h, l1_ffn2_norm)

    h = ffn(h, l2_ffn1_norm)
    h = _fox_layer(h, l2_mix_norm, l2_mix_win, l2_fox_bf, l2_mix_wout, batch, seq)
    h = ffn(h, l2_ffn2_norm)

    h = ffn(h, l3_ffn1_norm)
    h = _gmlp_layer(h, l3_mix_norm, l3_mix_win, l3_gmlp_vnorm, l3_gmlp_ws, l3_gmlp_bs, l3_mix_wout)
    h = ffn(h, l3_ffn2_norm, final_norm)
    return h.reshape(batch, seq, d)
```

```python
import functools
import math

import jax
import jax.numpy as jnp
import numpy as np
from jax import lax
from jax.experimental import pallas as pl
from jax.experimental.pallas import tpu as pltpu

F32 = jnp.float32
BF16 = jnp.bfloat16

NORM_EPS = 1e-5
NEG_INF = -1e30
CHUNK = 128
GMLP_GROUPS = 16
SWA_HEAD_DIM = 64
SWA_Q_HEADS = 32
SWA_KV_HEADS = 4
SWA_Q_PER_KV = SWA_Q_HEADS // SWA_KV_HEADS
ROPE_THETA = 500000.0
ROPE_DIM = SWA_HEAD_DIM // 4
FOX_HEAD_DIM = 128
FOX_HEADS = 16

LANES = 128
V7X_VMEM_BYTES = 64 * 1024 * 1024
V7X_VMEM_USABLE_BYTES = 60 * 1024 * 1024


def _nbytes(shape, dtype):
    return math.prod(shape) * jnp.dtype(dtype).itemsize


def _compiler_params(semantics, pipelined, resident):
    need = 2 * sum(_nbytes(s, d) for s, d in pipelined) + sum(_nbytes(s, d) for s, d in resident)
    limit = min(V7X_VMEM_USABLE_BYTES, max(32 * 1024 * 1024, need + need // 4))
    return pltpu.CompilerParams(dimension_semantics=semantics, vmem_limit_bytes=limit)


def _rms(x, gain):
    return x * lax.rsqrt(jnp.mean(x * x, axis=-1, keepdims=True) + NORM_EPS) * gain


def _ffn_kernel(*refs, n_ff_tiles, final_norm, cast_next, n_cast_extra):
    refs = list(refs)
    x_ref, g_ref, wg_ref, wu_ref, wo_ref = refs[:5]
    del refs[:5]
    gf_ref = refs.pop(0) if final_norm else None
    if cast_next:
        wi_next_ref, wo_next_ref = refs[:2]
        del refs[:2]
    extra_refs = refs[:n_cast_extra]
    del refs[:n_cast_extra]
    o_ref = refs.pop(0)
    if cast_next:
        wi_next_bf16_ref, wo_next_bf16_ref = refs[:2]
        del refs[:2]
    extra_bf16_refs = refs[:n_cast_extra]
    del refs[:n_cast_extra]
    (xn_ref,) = refs
    j = pl.program_id(1)

    @pl.when(j == 0)
    def _():
        x = x_ref[...]
        xn_ref[...] = _rms(x, g_ref[...]).astype(BF16)
        o_ref[...] = x

    xn = xn_ref[...]
    gate = jnp.dot(xn, wg_ref[...], preferred_element_type=F32)
    up = jnp.dot(xn, wu_ref[...], preferred_element_type=F32)
    act = (gate * jax.nn.sigmoid(gate) * up).astype(BF16)
    o_ref[...] += jnp.dot(act, wo_ref[...], preferred_element_type=F32)

    if cast_next:
        wi_next_bf16_ref[...] = wi_next_ref[...].astype(BF16)
        wo_next_bf16_ref[...] = (0.5 * wo_next_ref[...]).astype(BF16)
    for src_ref, dst_ref in zip(extra_refs, extra_bf16_refs):
        dst_ref[...] = src_ref[...].astype(BF16)

    if final_norm:
        @pl.when(j == n_ff_tiles - 1)
        def _():
            o_ref[...] = _rms(o_ref[...], gf_ref[...])


BF16_SUBLANES = 16


def _ffn(h, gain, wi, wo_half, final_gain=None, next_weights=None, cast_extra=(), *, tm=1024, tf=512):
    wo = wo_half
    t, d = h.shape
    tm = min(tm, t)
    d_ff = wo.shape[0]
    n_ff_tiles = d_ff // tf
    n_row_tiles = t // tm
    assert t % tm == 0 and d_ff % tf == 0
    in_specs = [
        pl.BlockSpec((tm, d), lambda i, j: (i, 0)),
        pl.BlockSpec((1, d), lambda i, j: (0, 0)),
        pl.BlockSpec((d, tf), lambda i, j: (0, j)),
        pl.BlockSpec((d, tf), lambda i, j: (0, j + n_ff_tiles)),
        pl.BlockSpec((tf, d), lambda i, j: (j, 0)),
    ]
    args = [h, gain.reshape(1, d), wi, wi, wo]
    out_specs = [pl.BlockSpec((tm, d), lambda i, j: (i, 0))]
    out_shape = [jax.ShapeDtypeStruct((t, d), F32)]
    pipelined = [((tm, d), F32), ((tm, d), F32), ((d, tf), BF16), ((d, tf), BF16), ((tf, d), BF16)]
    if final_gain is not None:
        in_specs.append(pl.BlockSpec((1, d), lambda i, j: (0, 0)))
        args.append(final_gain.reshape(1, d))
    if next_weights is not None:
        wi_next, wo_next = next_weights
        assert wi_next.shape == wi.shape and wo_next.shape == wo.shape
        assert d % n_row_tiles == 0 and wi.shape[1] % n_ff_tiles == 0
        wi_block = (d // n_row_tiles, wi.shape[1] // n_ff_tiles)
        wo_block = (d_ff // n_ff_tiles, d // n_row_tiles)
        wi_spec = pl.BlockSpec(wi_block, lambda i, j: (i, j))
        wo_spec = pl.BlockSpec(wo_block, lambda i, j: (j, i))
        in_specs += [wi_spec, wo_spec]
        args += [wi_next, wo_next]
        out_specs += [wi_spec, wo_spec]
        out_shape += [jax.ShapeDtypeStruct(wi.shape, BF16), jax.ShapeDtypeStruct(wo.shape, BF16)]
        pipelined += [(wi_block, F32), (wo_block, F32), (wi_block, BF16), (wo_block, BF16)]
    extra_in_specs, extra_out_specs, extra_out_shape = [], [], []
    for w in cast_extra:
        rows, cols = w.shape
        n_sub = max(s for s in range(1, n_ff_tiles + 1)
                    if rows % (n_row_tiles * s * BF16_SUBLANES) == 0)
        block = (rows // (n_row_tiles * n_sub), cols)
        spec = pl.BlockSpec(block, lambda i, j, n_sub=n_sub: (i * n_sub + jnp.minimum(j, n_sub - 1), 0))
        extra_in_specs.append(spec)
        extra_out_specs.append(spec)
        extra_out_shape.append(jax.ShapeDtypeStruct(w.shape, BF16))
        pipelined += [(block, F32), (block, BF16)]
    in_specs += extra_in_specs
    args += list(cast_extra)
    out_specs += extra_out_specs
    out_shape += extra_out_shape
    outs = pl.pallas_call(
        functools.partial(_ffn_kernel, n_ff_tiles=n_ff_tiles, final_norm=final_gain is not None,
                          cast_next=next_weights is not None, n_cast_extra=len(cast_extra)),
        grid=(n_row_tiles, n_ff_tiles),
        in_specs=in_specs,
        out_specs=out_specs,
        out_shape=out_shape,
        scratch_shapes=[pltpu.VMEM((tm, d), BF16)],
        compiler_params=_compiler_params(
            ("parallel", "arbitrary"), pipelined=pipelined,
            resident=[((tm, d), BF16), ((tm, tf), F32), ((tm, tf), F32), ((tm, tf), F32)]),
        name="ffn",
    )(*args)
    n_next = 0 if next_weights is None else 2
    return outs[0], (tuple(outs[1:3]) if n_next else None), tuple(outs[1 + n_next:])


def _norm_matmul_kernel(*refs, gelu, side):
    if side:
        x_ref, g_ref, w_ref, w2_ref, o_ref, o2_ref, xn_ref = refs
    else:
        x_ref, g_ref, w_ref, o_ref, xn_ref = refs
    j = pl.program_id(1)

    @pl.when(j == 0)
    def _():
        xn = _rms(x_ref[...], g_ref[...]).astype(BF16)
        xn_ref[...] = xn
        if side:
            o2_ref[...] = jnp.dot(xn, w2_ref[...], preferred_element_type=F32)

    y = jnp.dot(xn_ref[...], w_ref[...], preferred_element_type=F32)
    if gelu:
        y = 0.5 * y * (1.0 + lax.erf(y * np.float32(math.sqrt(0.5))))
    o_ref[...] = y.astype(o_ref.dtype)


def _norm_matmul(h, gain, w, w_side=None, *, n_cols=None, gelu=False, tm=1024, tn=1024):
    t, d = h.shape
    tm = min(tm, t)
    n = w.shape[1] if n_cols is None else n_cols
    assert t % tm == 0 and n % tn == 0 and n <= w.shape[1]
    side = w_side is not None
    in_specs = [
        pl.BlockSpec((tm, d), lambda i, j: (i, 0)),
        pl.BlockSpec((1, d), lambda i, j: (0, 0)),
        pl.BlockSpec((d, tn), lambda i, j: (0, j)),
    ]
    args = [h, gain.reshape(1, d), w]
    out_specs = pl.BlockSpec((tm, tn), lambda i, j: (i, j))
    out_shape = jax.ShapeDtypeStruct((t, n), BF16)
    pipelined = [((tm, d), F32), ((d, tn), BF16), ((tm, tn), BF16)]
    if side:
        n2 = w_side.shape[1]
        in_specs.append(pl.BlockSpec((d, n2), lambda i, j: (0, 0)))
        args.append(w_side)
        out_specs = [out_specs, pl.BlockSpec((tm, n2), lambda i, j: (i, 0))]
        out_shape = [out_shape, jax.ShapeDtypeStruct((t, n2), F32)]
        pipelined += [((d, n2), BF16), ((tm, n2), F32)]
    return pl.pallas_call(
        functools.partial(_norm_matmul_kernel, gelu=gelu, side=side),
        grid=(t // tm, n // tn),
        in_specs=in_specs,
        out_specs=out_specs,
        out_shape=out_shape,
        scratch_shapes=[pltpu.VMEM((tm, d), BF16)],
        compiler_params=_compiler_params(
            ("parallel", "arbitrary"), pipelined=pipelined,
            resident=[((tm, d), BF16), ((tm, tn), F32), ((tm, tn), F32)]),
        name="norm_matmul",
    )(*args)


def _matmul_residual_kernel(y_ref, w_ref, h_ref, o_ref):
    o_ref[...] = h_ref[...] + jnp.dot(y_ref[...], w_ref[...], preferred_element_type=F32)


def _matmul_residual(y, w, h, *, tm=512):
    t, k = y.shape
    d = w.shape[1]
    assert t % tm == 0
    return pl.pallas_call(
        _matmul_residual_kernel,
        grid=(t // tm,),
        in_specs=[
            pl.BlockSpec((tm, k), lambda i: (i, 0)),
            pl.BlockSpec((k, d), lambda i: (0, 0)),
            pl.BlockSpec((tm, d), lambda i: (i, 0)),
        ],
        out_specs=pl.BlockSpec((tm, d), lambda i: (i, 0)),
        out_shape=jax.ShapeDtypeStruct((t, d), F32),
        compiler_params=_compiler_params(
            ("parallel",),
            pipelined=[((tm, k), BF16), ((k, d), BF16), ((tm, d), F32), ((tm, d), F32)],
            resident=[((tm, d), F32)]),
        name="matmul_residual",
    )(y, w, h)


def _gmlp_mix_kernel(u_ref, v_ref, vg_ref, ws_ref, bias_ref, wout_ref, h_ref, o_ref, wsm_ref, y_ref,
                     *, n_chunks):
    @pl.when(pl.program_id(0) == 0)
    def _():
        row = lax.broadcasted_iota(jnp.int32, (CHUNK, CHUNK), 0)
        col = lax.broadcasted_iota(jnp.int32, (CHUNK, CHUNK), 1)
        for g in range(GMLP_GROUPS):
            wsm_ref[g] = jnp.where(col <= row, ws_ref[g], 0.0).astype(BF16)

    for c in range(n_chunks):
        rows = slice(c * CHUNK, (c + 1) * CHUNK)
        vn = _rms(v_ref[rows, :].astype(F32), vg_ref[...]).astype(BF16)
        for g in range(GMLP_GROUPS):
            cols = slice(g * LANES, (g + 1) * LANES)
            mixed = jnp.dot(wsm_ref[g], vn[:, cols], preferred_element_type=F32) + bias_ref[:, cols]
            y_ref[rows, cols] = (u_ref[rows, cols].astype(F32) * mixed).astype(BF16)

    o_ref[...] = h_ref[...] + jnp.dot(y_ref[...], wout_ref[...], preferred_element_type=F32)


def _gmlp_mix(z, v_gain, ws, bs, w_out, h, *, tm=512):
    t, d = h.shape
    tm = min(tm, t)
    width = z.shape[1] // 2
    assert width == GMLP_GROUPS * LANES and t % tm == 0 and tm % CHUNK == 0
    n_width_blocks = 1
    bias = jnp.repeat(bs.T, LANES, axis=1)
    return pl.pallas_call(
        functools.partial(_gmlp_mix_kernel, n_chunks=tm // CHUNK),
        grid=(t // tm,),
        in_specs=[
            pl.BlockSpec((tm, width), lambda i: (i, 0)),
            pl.BlockSpec((tm, width), lambda i: (i, n_width_blocks)),
            pl.BlockSpec((1, width), lambda i: (0, 0)),
            pl.BlockSpec((GMLP_GROUPS, CHUNK, CHUNK), lambda i: (0, 0, 0)),
            pl.BlockSpec((CHUNK, width), lambda i: (0, 0)),
            pl.BlockSpec((width, d), lambda i: (0, 0)),
            pl.BlockSpec((tm, d), lambda i: (i, 0)),
        ],
        out_specs=pl.BlockSpec((tm, d), lambda i: (i, 0)),
        out_shape=jax.ShapeDtypeStruct((t, d), F32),
        scratch_shapes=[pltpu.VMEM((GMLP_GROUPS, CHUNK, CHUNK), BF16), pltpu.VMEM((tm, width), BF16)],
        compiler_params=_compiler_params(
            ("arbitrary",),
            pipelined=[((tm, width), BF16), ((tm, width), BF16), ((1, width), F32),
                       ((GMLP_GROUPS, CHUNK, CHUNK), F32), ((CHUNK, width), F32), ((width, d), BF16),
                       ((tm, d), F32), ((tm, d), F32)],
            resident=[((GMLP_GROUPS, CHUNK, CHUNK), BF16), ((tm, width), BF16), ((tm, d), F32)]),
        name="gmlp_mix",
    )(z, z, v_gain.reshape(1, width), ws, bias, w_out, h)


def _rope_tables(seq):
    half = ROPE_DIM // 2
    inv_freq = ROPE_THETA ** (-(jnp.arange(half, dtype=F32) * 2.0 / ROPE_DIM))
    ang = jnp.arange(seq, dtype=F32)[:, None] * inv_freq[None, :]
    cos, sin = jnp.cos(ang), jnp.sin(ang)
    rest = jnp.zeros((seq, SWA_HEAD_DIM - ROPE_DIM), F32)
    cos_h = jnp.concatenate([cos, cos, rest + 1.0], axis=1)
    sin_h = jnp.concatenate([sin, sin, rest], axis=1)
    rot = np.zeros((LANES, LANES), np.float32)
    for i in range(LANES):
        if i % SWA_HEAD_DIM < half:
            rot[i + half, i] = -1.0
        elif i % SWA_HEAD_DIM < ROPE_DIM:
            rot[i - half, i] = 1.0
    return (jnp.concatenate([cos_h, cos_h], axis=1), jnp.concatenate([sin_h, sin_h], axis=1),
            jnp.asarray(rot, BF16))


def _swa_kernel(sinks_ref, q_ref, kc_ref, kp_ref, vc_ref, vp_ref, cos_ref, sin_ref, rot_ref, bias_ref, o_ref):
    n = pl.program_id(1)
    log2e = np.float32(math.log2(math.e))
    cur = pl.ds(pl.multiple_of(n * CHUNK, CHUNK), CHUNK)
    prev = pl.ds(pl.multiple_of(jnp.maximum(n - 1, 0) * CHUNK, CHUNK), CHUNK)
    low_lanes = lax.broadcasted_iota(jnp.int32, (CHUNK, LANES), 1) < SWA_HEAD_DIM
    pairs = SWA_Q_PER_KV // 2
    q_scale = np.float32(SWA_HEAD_DIM ** -0.5 * math.log2(math.e))

    for g in range(SWA_KV_HEADS):
        lanes_g = slice(g * LANES, (g + 1) * LANES)
        pieces = [q_ref[0, :, (g * pairs + pair) * LANES:(g * pairs + pair + 1) * LANES] for pair in range(pairs)]
        pieces += [kc_ref[0, :, lanes_g], kp_ref[0, :, lanes_g]]
        partner = jnp.dot(jnp.concatenate(pieces, axis=0), rot_ref[...], preferred_element_type=F32)

        def rope(idx, rows):
            return (pieces[idx].astype(F32) * cos_ref[rows, :]
                    + partner[idx * CHUNK:(idx + 1) * CHUNK] * sin_ref[rows, :])

        k = jnp.concatenate([rope(pairs + 1, prev), rope(pairs, cur)], axis=0).astype(BF16)
        v = jnp.concatenate([vp_ref[0, :, lanes_g], vc_ref[0, :, lanes_g]], axis=0)
        stacked = []
        for pair in range(pairs):
            q = rope(pair, cur) * q_scale
            stacked += [jnp.where(low_lanes, q, 0.0), jnp.where(low_lanes, 0.0, q)]
        q8 = jnp.concatenate(stacked, axis=0).astype(BF16)
        s = lax.dot_general(q8, k, (((1,), (1,)), ((), ())), preferred_element_type=F32)
        probs, inv_denoms = [], []
        for h in range(SWA_Q_PER_KV):
            rows = slice(h * CHUNK, (h + 1) * CHUNK)
            s_h = s[rows] + bias_ref[0]
            sink = sinks_ref[g * SWA_Q_PER_KV + h] * log2e
            m = jnp.maximum(jnp.max(s_h, axis=1, keepdims=True), sink)
            p = jnp.exp2(s_h - m)
            inv_denoms.append(1.0 / (jnp.sum(p, axis=1, keepdims=True) + jnp.exp2(sink - m)))
            probs.append(p.astype(BF16))
        o8 = jnp.dot(jnp.concatenate(probs, axis=0), v, preferred_element_type=F32)
        for pair in range(pairs):
            lanes_q = slice((g * pairs + pair) * LANES, (g * pairs + pair + 1) * LANES)
            lo = o8[2 * pair * CHUNK:(2 * pair + 1) * CHUNK] * inv_denoms[2 * pair]
            hi = o8[(2 * pair + 1) * CHUNK:(2 * pair + 2) * CHUNK] * inv_denoms[2 * pair + 1]
            o_ref[0, :, lanes_q] = jnp.where(low_lanes, lo, hi).astype(o_ref.dtype)


def _swa_attention(qkv, sinks, batch, seq):
    nb = seq // CHUNK
    qw = SWA_Q_HEADS * SWA_HEAD_DIM
    kw = SWA_KV_HEADS * LANES
    qkv = qkv.reshape(batch, seq, qw + 2 * kw)
    cos, sin, rot = _rope_tables(seq)
    k_blk = qw // kw
    table_spec = pl.BlockSpec((seq, LANES), lambda b, n: (0, 0))
    r = np.arange(CHUNK)[:, None]
    c = np.arange(2 * CHUNK)[None, :]
    band = (c > r) & (c <= r + CHUNK)
    bias = jnp.asarray(np.where(np.stack([band & (c >= CHUNK), band]), 0.0, NEG_INF), F32)
    return pl.pallas_call(
        _swa_kernel,
        grid=(batch, nb),
        in_specs=[
            pl.BlockSpec(memory_space=pltpu.SMEM),
            pl.BlockSpec((1, CHUNK, qw), lambda b, n: (b, n, 0)),
            pl.BlockSpec((1, CHUNK, kw), lambda b, n: (b, n, k_blk)),
            pl.BlockSpec((1, CHUNK, kw), lambda b, n: (b, jnp.maximum(n - 1, 0), k_blk)),
            pl.BlockSpec((1, CHUNK, kw), lambda b, n: (b, n, k_blk + 1)),
            pl.BlockSpec((1, CHUNK, kw), lambda b, n: (b, jnp.maximum(n - 1, 0), k_blk + 1)),
            table_spec, table_spec, pl.BlockSpec((LANES, LANES), lambda b, n: (0, 0)),
            pl.BlockSpec((1, CHUNK, 2 * CHUNK), lambda b, n: (jnp.minimum(n, 1), 0, 0)),
        ],
        out_specs=pl.BlockSpec((1, CHUNK, qw), lambda b, n: (b, n, 0)),
        out_shape=jax.ShapeDtypeStruct((batch, seq, qw), BF16),
        compiler_params=_compiler_params(
            ("parallel", "arbitrary"),
            pipelined=[((CHUNK, qw), BF16)] * 2 + [((CHUNK, kw), BF16)] * 4 + [((seq, LANES), F32)] * 2
                      + [((CHUNK, 2 * CHUNK), F32)],
            resident=[((SWA_Q_PER_KV * CHUNK, 2 * CHUNK), F32)] * 6),
        name="swa_attention",
    )(sinks, qkv, qkv, qkv, qkv, qkv, cos, sin, rot, bias).reshape(batch * seq, qw)


FOX_CUMSUM_CHUNK = 256


def _fox_decay_kernel(fl_ref, bf_ref, o_ref, *, n_chunks):
    cc = FOX_CUMSUM_CHUNK
    row = lax.broadcasted_iota(jnp.int32, (cc, cc), 0)
    col = lax.broadcasted_iota(jnp.int32, (cc, cc), 1)
    lower = jnp.where(col <= row, 1.0, 0.0).astype(BF16)
    carry = jnp.zeros((1, LANES), F32)
    for c in range(n_chunks):
        rows = slice(c * cc, (c + 1) * cc)
        x = jax.nn.log_sigmoid(fl_ref[0, rows, :] + bf_ref[...])
        hi = x.astype(BF16)
        rest = x - hi.astype(F32)
        mid = rest.astype(BF16)
        lo = (rest - mid.astype(F32)).astype(BF16)
        cs = (jnp.dot(lower, hi, preferred_element_type=F32) + jnp.dot(lower, mid, preferred_element_type=F32)
              + jnp.dot(lower, lo, preferred_element_type=F32)) + carry
        o_ref[0, rows, :] = cs
        carry = cs[cc - 1:cc, :]


def _fox_decay(fl, b_f, batch, seq):
    assert seq % FOX_CUMSUM_CHUNK == 0
    bf = jnp.zeros((1, LANES), F32).at[0, :FOX_HEADS].set(b_f)
    return pl.pallas_call(
        functools.partial(_fox_decay_kernel, n_chunks=seq // FOX_CUMSUM_CHUNK),
        grid=(batch,),
        in_specs=[pl.BlockSpec((1, seq, LANES), lambda b: (b, 0, 0)), pl.BlockSpec((1, LANES), lambda b: (0, 0))],
        out_specs=pl.BlockSpec((1, seq, LANES), lambda b: (b, 0, 0)),
        out_shape=jax.ShapeDtypeStruct((batch, seq, LANES), F32),
        compiler_params=_compiler_params(
            ("parallel",), pipelined=[((seq, LANES), F32)] * 2, resident=[((seq, LANES), F32)]),
        name="fox_decay",
    )(fl.reshape(batch, seq, LANES), bf)


def _fox_kernel(q_ref, k_ref, v_ref, dq_ref, dk_ref, o_ref, *, tq, n_tiles):
    head = pl.program_id(1)
    log2e = np.float32(math.log2(math.e))
    scale = np.float32(FOX_HEAD_DIM ** -0.5 * math.log2(math.e))
    head_lane = lax.broadcasted_iota(jnp.int32, (1, LANES), 1) == head
    row = lax.broadcasted_iota(jnp.int32, (tq, tq), 0)
    col = lax.broadcasted_iota(jnp.int32, (tq, tq), 1)
    causal = col <= row

    for i in range(n_tiles):
        rows = slice(i * tq, (i + 1) * tq)
        dq = jnp.sum(jnp.where(head_lane, dq_ref[0, rows, :], 0.0), axis=1, keepdims=True) * log2e
        q = q_ref[0, rows, :]
        m = l = acc = None
        for kt in range(i + 1):
            keys = slice(kt * tq, (kt + 1) * tq)
            s = lax.dot_general(q, k_ref[0, keys, :], (((1,), (1,)), ((), ())), preferred_element_type=F32) * scale
            s = s + dq - dk_ref[0, 0, kt:kt + 1, :] * log2e
            if kt == i:
                s = jnp.where(causal, s, NEG_INF)
            m_tile = jnp.max(s, axis=1, keepdims=True)
            m_new = m_tile if m is None else jnp.maximum(m, m_tile)
            p = jnp.exp2(s - m_new)
            l_tile = jnp.sum(p, axis=1, keepdims=True)
            pv = jnp.dot(p.astype(BF16), v_ref[0, keys, :], preferred_element_type=F32)
            if m is None:
                l, acc = l_tile, pv
            else:
                alpha = jnp.exp2(m - m_new)
                l, acc = alpha * l + l_tile, alpha * acc + pv
            m = m_new
        o_ref[0, rows, :] = (acc / l).astype(o_ref.dtype)


def _fox_attention(qkv, dec, batch, seq, *, tq=512):
    width = FOX_HEADS * FOX_HEAD_DIM
    tq = min(tq, seq)
    assert seq % tq == 0
    qkv = qkv.reshape(batch, seq, 3 * width)
    dec_k = dec[:, :, :FOX_HEADS].transpose(0, 2, 1).reshape(batch, FOX_HEADS, seq // tq, tq)

    def head_spec(first_block):
        return pl.BlockSpec((1, seq, FOX_HEAD_DIM), lambda b, h: (b, 0, first_block + h))

    return pl.pallas_call(
        functools.partial(_fox_kernel, tq=tq, n_tiles=seq // tq),
        grid=(batch, FOX_HEADS),
        in_specs=[
            head_spec(0), head_spec(FOX_HEADS), head_spec(2 * FOX_HEADS),
            pl.BlockSpec((1, seq, LANES), lambda b, h: (b, 0, 0)),
            pl.BlockSpec((1, 1, seq // tq, tq), lambda b, h: (b, h, 0, 0)),
        ],
        out_specs=head_spec(0),
        out_shape=jax.ShapeDtypeStruct((batch, seq, width), BF16),
        compiler_params=_compiler_params(
            ("parallel", "parallel"),
            pipelined=[((seq, FOX_HEAD_DIM), BF16)] * 4 + [((seq, LANES), F32), ((seq // tq, tq), F32)],
            resident=[((tq, tq), F32)] * 12),
        name="fox_attention",
    )(qkv, qkv, qkv, dec, dec_k).reshape(batch * seq, width)


def _gmlp_layer(h, norm, w_in, v_gain, ws, bs, w_out):
    z = _norm_matmul(h, norm, w_in, gelu=True)
    return _gmlp_mix(z, v_gain, ws, bs, w_out, h)


def _swa_layer(h, norm, w_in, sinks, w_out, batch, seq):
    d = h.shape[1]
    qw = SWA_Q_HEADS * SWA_HEAD_DIM
    kvw = SWA_KV_HEADS * SWA_HEAD_DIM

    def duplicate_heads(w):
        w = w.reshape(d, SWA_KV_HEADS, 1, SWA_HEAD_DIM)
        return jnp.broadcast_to(w, (d, SWA_KV_HEADS, 2, SWA_HEAD_DIM)).reshape(d, SWA_KV_HEADS * LANES)

    w = jnp.concatenate([w_in[:, :qw], duplicate_heads(w_in[:, qw:qw + kvw]),
                         duplicate_heads(w_in[:, qw + kvw:])], axis=1)
    qkv = _norm_matmul(h, norm, w)
    o = _swa_attention(qkv, sinks, batch, seq)
    return _matmul_residual(o, w_out, h)


def _fox_layer(h, norm, w_in, b_f, w_out, batch, seq):
    d = h.shape[1]
    width = FOX_HEADS * FOX_HEAD_DIM
    w_f = jnp.zeros((d, LANES), BF16).at[:, :FOX_HEADS].set(w_in[:, 3 * width:])
    qkv, fl = _norm_matmul(h, norm, w_in, w_f, n_cols=3 * width)
    dec = _fox_decay(fl, b_f, batch, seq)
    o = _fox_attention(qkv, dec, batch, seq)
    return _matmul_residual(o, w_out, h)


def kernel(x, l0_ffn1_norm, l0_ffn1_wi, l0_ffn1_wo, l0_mix_norm, l0_mix_win, l0_gmlp_vnorm, l0_gmlp_ws, l0_gmlp_bs, l0_mix_wout, l0_ffn2_norm, l0_ffn2_wi, l0_ffn2_wo, l1_ffn1_norm, l1_ffn1_wi, l1_ffn1_wo, l1_mix_norm, l1_mix_win, l1_swa_sinks, l1_mix_wout, l1_ffn2_norm, l1_ffn2_wi, l1_ffn2_wo, l2_ffn1_norm, l2_ffn1_wi, l2_ffn1_wo, l2_mix_norm, l2_mix_win, l2_fox_bf, l2_mix_wout, l2_ffn2_norm, l2_ffn2_wi, l2_ffn2_wo, l3_ffn1_norm, l3_ffn1_wi, l3_ffn1_wo, l3_mix_norm, l3_mix_win, l3_gmlp_vnorm, l3_gmlp_ws, l3_gmlp_bs, l3_mix_wout, l3_ffn2_norm, l3_ffn2_wi, l3_ffn2_wo, final_norm):
    batch, seq, d = x.shape
    h = x.reshape(batch * seq, d)

    ffn_weights = [(l0_ffn1_wi, l0_ffn1_wo), (l0_ffn2_wi, l0_ffn2_wo), (l1_ffn1_wi, l1_ffn1_wo),
                   (l1_ffn2_wi, l1_ffn2_wo), (l2_ffn1_wi, l2_ffn1_wo), (l2_ffn2_wi, l2_ffn2_wo),
                   (l3_ffn1_wi, l3_ffn1_wo), (l3_ffn2_wi, l3_ffn2_wo)]
    cast = [(ffn_weights[0][0].astype(BF16), (0.5 * ffn_weights[0][1]).astype(BF16))]

    def ffn(h, norm, final_gain=None, cast_extra=()):
        index = len(cast) - 1
        wi, wo_half = cast[index]
        next_weights = ffn_weights[index + 1] if index + 1 < len(ffn_weights) else None
        h, cast_next, extra = _ffn(h, norm, wi, wo_half, final_gain, next_weights, cast_extra)
        cast.append(cast_next)
        return h, extra

    h, (w_in, w_out) = ffn(h, l0_ffn1_norm, cast_extra=(l0_mix_win, l0_mix_wout))
    h = _gmlp_layer(h, l0_mix_norm, w_in, l0_gmlp_vnorm, l0_gmlp_ws, l0_gmlp_bs, w_out)
    h, _ = ffn(h, l0_ffn2_norm)

    h, (w_in, w_out) = ffn(h, l1_ffn1_norm, cast_extra=(l1_mix_win, l1_mix_wout))
    h = _swa_layer(h, l1_mix_norm, w_in, l1_swa_sinks, w_out, batch, seq)
    h, _ = ffn(h, l1_ffn2_norm)

    h, (w_in, w_out) = ffn(h, l2_ffn1_norm, cast_extra=(l2_mix_win, l2_mix_wout))
    h = _fox_layer(h, l2_mix_norm, w_in, l2_fox_bf, w_out, batch, seq)
    h, _ = ffn(h, l2_ffn2_norm)

    h, (w_in, w_out) = ffn(h, l3_ffn1_norm, cast_extra=(l3_mix_win, l3_mix_wout))
    h = _gmlp_layer(h, l3_mix_norm, w_in, l3_gmlp_vnorm, l3_gmlp_ws, l3_gmlp_bs, w_out)
    h, _ = ffn(h, l3_ffn2_norm, final_norm)
    return h.reshape(batch, seq, d)
```

```python
import functools
import math

import jax
import jax.numpy as jnp
import numpy as np
from jax import lax
from jax.experimental import pallas as pl
from jax.experimental.pallas import tpu as pltpu

F32 = jnp.float32
BF16 = jnp.bfloat16

NORM_EPS = 1e-5
NEG_INF = -1e30
CHUNK = 128
GMLP_GROUPS = 16
SWA_HEAD_DIM = 64
SWA_Q_HEADS = 32
SWA_KV_HEADS = 4
SWA_Q_PER_KV = SWA_Q_HEADS // SWA_KV_HEADS
ROPE_THETA = 500000.0
ROPE_DIM = SWA_HEAD_DIM // 4
FOX_HEAD_DIM = 128
FOX_HEADS = 16

LANES = 128
V7X_VMEM_BYTES = 64 * 1024 * 1024
V7X_VMEM_USABLE_BYTES = 60 * 1024 * 1024


def _nbytes(shape, dtype):
    return math.prod(shape) * jnp.dtype(dtype).itemsize


def _compiler_params(semantics, pipelined, resident):
    need = 2 * sum(_nbytes(s, d) for s, d in pipelined) + sum(_nbytes(s, d) for s, d in resident)
    limit = min(V7X_VMEM_USABLE_BYTES, max(32 * 1024 * 1024, need + need // 4))
    return pltpu.CompilerParams(dimension_semantics=semantics, vmem_limit_bytes=limit)


def _rms(x, gain):
    return x * lax.rsqrt(jnp.mean(x * x, axis=-1, keepdims=True) + NORM_EPS) * gain


def _rms_prepare(x, rinv_ref, xb_ref):
    rinv = lax.rsqrt(jnp.mean(x * x, axis=-1, keepdims=True) + NORM_EPS)
    rinv_ref[...] = jnp.broadcast_to(rinv, rinv_ref.shape)
    xb_ref[...] = x.astype(BF16)


def _row_scale(y, rinv):
    return y * jnp.concatenate([rinv] * (y.shape[1] // LANES), axis=1)


def _ffn_kernel(*refs, n_ff_tiles, final_norm, cast_next, cast_extra_gains):
    refs = list(refs)
    x_ref, wg_ref, wu_ref, wo_ref = refs[:4]
    del refs[:4]
    gf_ref = refs.pop(0) if final_norm else None
    if cast_next:
        wi_next_ref, wi_next_gain_ref, wo_next_ref = refs[:3]
        del refs[:3]
    extra_refs = []
    for has_gain in cast_extra_gains:
        extra_refs.append((refs.pop(0), refs.pop(0) if has_gain else None))
    o_ref = refs.pop(0)
    if cast_next:
        wi_next_bf16_ref, wo_next_bf16_ref = refs[:2]
        del refs[:2]
    extra_bf16_refs = refs[:len(cast_extra_gains)]
    del refs[:len(cast_extra_gains)]
    xb_ref, rinv_ref = refs
    j = pl.program_id(1)

    @pl.when(j == 0)
    def _():
        x = x_ref[...]
        _rms_prepare(x, rinv_ref, xb_ref)
        o_ref[...] = x

    xb = xb_ref[...]
    rinv = rinv_ref[...]
    gate = _row_scale(jnp.dot(xb, wg_ref[...], preferred_element_type=F32), rinv)
    up = _row_scale(jnp.dot(xb, wu_ref[...], preferred_element_type=F32), rinv)
    act = (gate * jax.nn.sigmoid(gate) * up).astype(BF16)
    o_ref[...] += jnp.dot(act, wo_ref[...], preferred_element_type=F32)

    if cast_next:
        wi_next_bf16_ref[...] = (wi_next_gain_ref[...] * wi_next_ref[...]).astype(BF16)
        wo_next_bf16_ref[...] = (0.5 * wo_next_ref[...]).astype(BF16)
    for (src_ref, gain_ref), dst_ref in zip(extra_refs, extra_bf16_refs):
        w = src_ref[...]
        dst_ref[...] = (w if gain_ref is None else gain_ref[...] * w).astype(BF16)

    if final_norm:
        @pl.when(j == n_ff_tiles - 1)
        def _():
            o_ref[...] = _rms(o_ref[...], gf_ref[...])


BF16_SUBLANES = 16


def _ffn(h, wi, wo_half, final_gain=None, next_weights=None, cast_extra=(), *, tm=1024, tf=512):
    wo = wo_half
    t, d = h.shape
    tm = min(tm, t)
    d_ff = wo.shape[0]
    n_ff_tiles = d_ff // tf
    n_row_tiles = t // tm
    assert t % tm == 0 and d_ff % tf == 0
    in_specs = [
        pl.BlockSpec((tm, d), lambda i, j: (i, 0)),
        pl.BlockSpec((d, tf), lambda i, j: (0, j)),
        pl.BlockSpec((d, tf), lambda i, j: (0, j + n_ff_tiles)),
        pl.BlockSpec((tf, d), lambda i, j: (j, 0)),
    ]
    args = [h, wi, wi, wo]
    out_specs = [pl.BlockSpec((tm, d), lambda i, j: (i, 0))]
    out_shape = [jax.ShapeDtypeStruct((t, d), F32)]
    pipelined = [((tm, d), F32), ((tm, d), F32), ((d, tf), BF16), ((d, tf), BF16), ((tf, d), BF16)]
    if final_gain is not None:
        in_specs.append(pl.BlockSpec((1, d), lambda i, j: (0, 0)))
        args.append(final_gain.reshape(1, d))
    if next_weights is not None:
        gain_next, wi_next, wo_next = next_weights
        assert wi_next.shape == wi.shape and wo_next.shape == wo.shape
        assert d % n_row_tiles == 0 and wi.shape[1] % n_ff_tiles == 0
        wi_block = (d // n_row_tiles, wi.shape[1] // n_ff_tiles)
        wo_block = (d_ff // n_ff_tiles, d // n_row_tiles)
        wi_spec = pl.BlockSpec(wi_block, lambda i, j: (i, j))
        wo_spec = pl.BlockSpec(wo_block, lambda i, j: (j, i))
        in_specs += [wi_spec, pl.BlockSpec((wi_block[0], 1), lambda i, j: (i, 0)), wo_spec]
        args += [wi_next, gain_next.reshape(d, 1), wo_next]
        out_specs += [wi_spec, wo_spec]
        out_shape += [jax.ShapeDtypeStruct(wi.shape, BF16), jax.ShapeDtypeStruct(wo.shape, BF16)]
        pipelined += [(wi_block, F32), (wo_block, F32), (wi_block, BF16), (wo_block, BF16), ((wi_block[0], LANES), F32)]
    extra_out_specs, extra_out_shape = [], []
    for w, row_gain in cast_extra:
        rows, cols = w.shape
        n_sub = max(s for s in range(1, n_ff_tiles + 1)
                    if rows % (n_row_tiles * s * BF16_SUBLANES) == 0)
        block_rows = rows // (n_row_tiles * n_sub)

        def row_block(i, j, n_sub=n_sub):
            return (i * n_sub + jnp.minimum(j, n_sub - 1), 0)

        spec = pl.BlockSpec((block_rows, cols), row_block)
        in_specs.append(spec)
        args.append(w)
        if row_gain is not None:
            in_specs.append(pl.BlockSpec((block_rows, 1), row_block))
            args.append(row_gain.reshape(rows, 1))
            pipelined.append(((block_rows, LANES), F32))
        extra_out_specs.append(spec)
        extra_out_shape.append(jax.ShapeDtypeStruct(w.shape, BF16))
        pipelined += [((block_rows, cols), F32), ((block_rows, cols), BF16)]
    out_specs += extra_out_specs
    out_shape += extra_out_shape
    outs = pl.pallas_call(
        functools.partial(_ffn_kernel, n_ff_tiles=n_ff_tiles, final_norm=final_gain is not None,
                          cast_next=next_weights is not None,
                          cast_extra_gains=tuple(g is not None for _, g in cast_extra)),
        grid=(n_row_tiles, n_ff_tiles),
        in_specs=in_specs,
        out_specs=out_specs,
        out_shape=out_shape,
        scratch_shapes=[pltpu.VMEM((tm, d), BF16), pltpu.VMEM((tm, LANES), F32)],
        compiler_params=_compiler_params(
            ("parallel", "arbitrary"), pipelined=pipelined,
            resident=[((tm, d), BF16), ((tm, LANES), F32), ((tm, tf), F32), ((tm, tf), F32), ((tm, tf), F32)]),
        name="ffn",
    )(*args)
    n_next = 0 if next_weights is None else 2
    return outs[0], (tuple(outs[1:3]) if n_next else None), tuple(outs[1 + n_next:])


def _norm_matmul_kernel(*refs, gelu, side, scaled_tiles, scale):
    if side:
        x_ref, w_ref, w2_ref, o_ref, o2_ref, xb_ref, rinv_ref = refs
    else:
        x_ref, w_ref, o_ref, xb_ref, rinv_ref = refs
    j = pl.program_id(1)

    @pl.when(j == 0)
    def _():
        _rms_prepare(x_ref[...], rinv_ref, xb_ref)
        if side:
            o2_ref[...] = _row_scale(jnp.dot(xb_ref[...], w2_ref[...], preferred_element_type=F32), rinv_ref[...])

    y = _row_scale(jnp.dot(xb_ref[...], w_ref[...], preferred_element_type=F32), rinv_ref[...])
    if gelu:
        y = 0.5 * y * (1.0 + lax.erf(y * np.float32(math.sqrt(0.5))))
    if scaled_tiles:
        y = y * jnp.where(j < scaled_tiles, np.float32(scale), np.float32(1.0))
    o_ref[...] = y.astype(o_ref.dtype)


def _norm_matmul(h, w, w_side=None, *, n_cols=None, gelu=False, scaled_cols=0, scale=1.0, tm=1024, tn=1024):
    t, d = h.shape
    tm = min(tm, t)
    n = w.shape[1] if n_cols is None else n_cols
    assert t % tm == 0 and n % tn == 0 and n <= w.shape[1] and scaled_cols % tn == 0
    side = w_side is not None
    in_specs = [
        pl.BlockSpec((tm, d), lambda i, j: (i, 0)),
        pl.BlockSpec((d, tn), lambda i, j: (0, j)),
    ]
    args = [h, w]
    out_specs = pl.BlockSpec((tm, tn), lambda i, j: (i, j))
    out_shape = jax.ShapeDtypeStruct((t, n), BF16)
    pipelined = [((tm, d), F32), ((d, tn), BF16), ((tm, tn), BF16)]
    if side:
        n2 = w_side.shape[1]
        assert n2 == LANES
        in_specs.append(pl.BlockSpec((d, n2), lambda i, j: (0, 0)))
        args.append(w_side)
        out_specs = [out_specs, pl.BlockSpec((tm, n2), lambda i, j: (i, 0))]
        out_shape = [out_shape, jax.ShapeDtypeStruct((t, n2), F32)]
        pipelined += [((d, n2), BF16), ((tm, n2), F32)]
    return pl.pallas_call(
        functools.partial(_norm_matmul_kernel, gelu=gelu, side=side, scaled_tiles=scaled_cols // tn, scale=scale),
        grid=(t // tm, n // tn),
        in_specs=in_specs,
        out_specs=out_specs,
        out_shape=out_shape,
        scratch_shapes=[pltpu.VMEM((tm, d), BF16), pltpu.VMEM((tm, LANES), F32)],
        compiler_params=_compiler_params(
            ("parallel", "arbitrary"), pipelined=pipelined,
            resident=[((tm, d), BF16), ((tm, LANES), F32), ((tm, tn), F32), ((tm, tn), F32)]),
        name="norm_matmul",
    )(*args)


def _matmul_residual_kernel(y_ref, w_ref, h_ref, o_ref):
    o_ref[...] = h_ref[...] + jnp.dot(y_ref[...], w_ref[...], preferred_element_type=F32)


def _matmul_residual(y, w, h, *, tm=512):
    t, k = y.shape
    d = w.shape[1]
    assert t % tm == 0
    return pl.pallas_call(
        _matmul_residual_kernel,
        grid=(t // tm,),
        in_specs=[
            pl.BlockSpec((tm, k), lambda i: (i, 0)),
            pl.BlockSpec((k, d), lambda i: (0, 0)),
            pl.BlockSpec((tm, d), lambda i: (i, 0)),
        ],
        out_specs=pl.BlockSpec((tm, d), lambda i: (i, 0)),
        out_shape=jax.ShapeDtypeStruct((t, d), F32),
        compiler_params=_compiler_params(
            ("parallel",),
            pipelined=[((tm, k), BF16), ((k, d), BF16), ((tm, d), F32), ((tm, d), F32)],
            resident=[((tm, d), F32)]),
        name="matmul_residual",
    )(y, w, h)


def _gmlp_mix_kernel(u_ref, v_ref, vg_ref, ws_ref, bias_ref, wout_ref, h_ref, o_ref, wsm_ref, y_ref,
                     *, n_chunks):
    @pl.when(pl.program_id(0) == 0)
    def _():
        row = lax.broadcasted_iota(jnp.int32, (CHUNK, CHUNK), 0)
        col = lax.broadcasted_iota(jnp.int32, (CHUNK, CHUNK), 1)
        for g in range(GMLP_GROUPS):
            wsm_ref[g] = jnp.where(col <= row, ws_ref[g], 0.0).astype(BF16)

    for c in range(n_chunks):
        rows = slice(c * CHUNK, (c + 1) * CHUNK)
        vn = _rms(v_ref[rows, :].astype(F32), vg_ref[...]).astype(BF16)
        for g in range(GMLP_GROUPS):
            cols = slice(g * LANES, (g + 1) * LANES)
            mixed = jnp.dot(wsm_ref[g], vn[:, cols], preferred_element_type=F32) + bias_ref[:, cols]
            y_ref[rows, cols] = (u_ref[rows, cols].astype(F32) * mixed).astype(BF16)

    o_ref[...] = h_ref[...] + jnp.dot(y_ref[...], wout_ref[...], preferred_element_type=F32)


def _gmlp_mix(z, v_gain, ws, bs, w_out, h, *, tm=512):
    t, d = h.shape
    tm = min(tm, t)
    width = z.shape[1] // 2
    assert width == GMLP_GROUPS * LANES and t % tm == 0 and tm % CHUNK == 0
    n_width_blocks = 1
    bias = jnp.repeat(bs.T, LANES, axis=1)
    return pl.pallas_call(
        functools.partial(_gmlp_mix_kernel, n_chunks=tm // CHUNK),
        grid=(t // tm,),
        in_specs=[
            pl.BlockSpec((tm, width), lambda i: (i, 0)),
            pl.BlockSpec((tm, width), lambda i: (i, n_width_blocks)),
            pl.BlockSpec((1, width), lambda i: (0, 0)),
            pl.BlockSpec((GMLP_GROUPS, CHUNK, CHUNK), lambda i: (0, 0, 0)),
            pl.BlockSpec((CHUNK, width), lambda i: (0, 0)),
            pl.BlockSpec((width, d), lambda i: (0, 0)),
            pl.BlockSpec((tm, d), lambda i: (i, 0)),
        ],
        out_specs=pl.BlockSpec((tm, d), lambda i: (i, 0)),
        out_shape=jax.ShapeDtypeStruct((t, d), F32),
        scratch_shapes=[pltpu.VMEM((GMLP_GROUPS, CHUNK, CHUNK), BF16), pltpu.VMEM((tm, width), BF16)],
        compiler_params=_compiler_params(
            ("arbitrary",),
            pipelined=[((tm, width), BF16), ((tm, width), BF16), ((1, width), F32),
                       ((GMLP_GROUPS, CHUNK, CHUNK), F32), ((CHUNK, width), F32), ((width, d), BF16),
                       ((tm, d), F32), ((tm, d), F32)],
            resident=[((GMLP_GROUPS, CHUNK, CHUNK), BF16), ((tm, width), BF16), ((tm, d), F32)]),
        name="gmlp_mix",
    )(z, z, v_gain.reshape(1, width), ws, bias, w_out, h)


def _rope_tables(seq):
    half = ROPE_DIM // 2
    inv_freq = ROPE_THETA ** (-(jnp.arange(half, dtype=F32) * 2.0 / ROPE_DIM))
    ang = jnp.arange(seq, dtype=F32)[:, None] * inv_freq[None, :]
    cos, sin = jnp.cos(ang), jnp.sin(ang)
    rest = jnp.zeros((seq, SWA_HEAD_DIM - ROPE_DIM), F32)
    cos_h = jnp.concatenate([cos, cos, rest + 1.0], axis=1)
    sin_h = jnp.concatenate([sin, sin, rest], axis=1)
    rot = np.zeros((LANES, LANES), np.float32)
    for i in range(LANES):
        if i % SWA_HEAD_DIM < half:
            rot[i + half, i] = -1.0
        elif i % SWA_HEAD_DIM < ROPE_DIM:
            rot[i - half, i] = 1.0
    return (jnp.concatenate([cos_h, cos_h], axis=1), jnp.concatenate([sin_h, sin_h], axis=1),
            jnp.asarray(rot, BF16))


def _swa_kernel(sinks_ref, q_ref, kc_ref, kp_ref, vc_ref, vp_ref, cos_ref, sin_ref, rot_ref, bias_ref, o_ref):
    n = pl.program_id(1)
    log2e = np.float32(math.log2(math.e))
    cur = pl.ds(pl.multiple_of(n * CHUNK, CHUNK), CHUNK)
    prev = pl.ds(pl.multiple_of(jnp.maximum(n - 1, 0) * CHUNK, CHUNK), CHUNK)
    low_lanes = lax.broadcasted_iota(jnp.int32, (CHUNK, LANES), 1) < SWA_HEAD_DIM
    pairs = SWA_Q_PER_KV // 2
    q_scale = np.float32(SWA_HEAD_DIM ** -0.5 * math.log2(math.e))

    for g in range(SWA_KV_HEADS):
        lanes_g = slice(g * LANES, (g + 1) * LANES)
        pieces = [q_ref[0, :, (g * pairs + pair) * LANES:(g * pairs + pair + 1) * LANES] for pair in range(pairs)]
        pieces += [kc_ref[0, :, lanes_g], kp_ref[0, :, lanes_g]]
        partner = jnp.dot(jnp.concatenate(pieces, axis=0), rot_ref[...], preferred_element_type=F32)

        def rope(idx, rows):
            return (pieces[idx].astype(F32) * cos_ref[rows, :]
                    + partner[idx * CHUNK:(idx + 1) * CHUNK] * sin_ref[rows, :])

        k = jnp.concatenate([rope(pairs + 1, prev), rope(pairs, cur)], axis=0).astype(BF16)
        v = jnp.concatenate([vp_ref[0, :, lanes_g], vc_ref[0, :, lanes_g]], axis=0)
        stacked = []
        for pair in range(pairs):
            q = rope(pair, cur) * q_scale
            stacked += [jnp.where(low_lanes, q, 0.0), jnp.where(low_lanes, 0.0, q)]
        q8 = jnp.concatenate(stacked, axis=0).astype(BF16)
        s = lax.dot_general(q8, k, (((1,), (1,)), ((), ())), preferred_element_type=F32)
        probs, inv_denoms = [], []
        for h in range(SWA_Q_PER_KV):
            rows = slice(h * CHUNK, (h + 1) * CHUNK)
            s_h = s[rows] + bias_ref[0]
            sink = sinks_ref[g * SWA_Q_PER_KV + h] * log2e
            m = jnp.maximum(jnp.max(s_h, axis=1, keepdims=True), sink)
            p = jnp.exp2(s_h - m)
            inv_denoms.append(1.0 / (jnp.sum(p, axis=1, keepdims=True) + jnp.exp2(sink - m)))
            probs.append(p.astype(BF16))
        o8 = jnp.dot(jnp.concatenate(probs, axis=0), v, preferred_element_type=F32)
        for pair in range(pairs):
            lanes_q = slice((g * pairs + pair) * LANES, (g * pairs + pair + 1) * LANES)
            lo = o8[2 * pair * CHUNK:(2 * pair + 1) * CHUNK] * inv_denoms[2 * pair]
            hi = o8[(2 * pair + 1) * CHUNK:(2 * pair + 2) * CHUNK] * inv_denoms[2 * pair + 1]
            o_ref[0, :, lanes_q] = jnp.where(low_lanes, lo, hi).astype(o_ref.dtype)


def _swa_attention(qkv, sinks, batch, seq):
    nb = seq // CHUNK
    qw = SWA_Q_HEADS * SWA_HEAD_DIM
    kw = SWA_KV_HEADS * LANES
    qkv = qkv.reshape(batch, seq, qw + 2 * kw)
    cos, sin, rot = _rope_tables(seq)
    k_blk = qw // kw
    table_spec = pl.BlockSpec((seq, LANES), lambda b, n: (0, 0))
    r = np.arange(CHUNK)[:, None]
    c = np.arange(2 * CHUNK)[None, :]
    band = (c > r) & (c <= r + CHUNK)
    bias = jnp.asarray(np.where(np.stack([band & (c >= CHUNK), band]), 0.0, NEG_INF), F32)
    return pl.pallas_call(
        _swa_kernel,
        grid=(batch, nb),
        in_specs=[
            pl.BlockSpec(memory_space=pltpu.SMEM),
            pl.BlockSpec((1, CHUNK, qw), lambda b, n: (b, n, 0)),
            pl.BlockSpec((1, CHUNK, kw), lambda b, n: (b, n, k_blk)),
            pl.BlockSpec((1, CHUNK, kw), lambda b, n: (b, jnp.maximum(n - 1, 0), k_blk)),
            pl.BlockSpec((1, CHUNK, kw), lambda b, n: (b, n, k_blk + 1)),
            pl.BlockSpec((1, CHUNK, kw), lambda b, n: (b, jnp.maximum(n - 1, 0), k_blk + 1)),
            table_spec, table_spec, pl.BlockSpec((LANES, LANES), lambda b, n: (0, 0)),
            pl.BlockSpec((1, CHUNK, 2 * CHUNK), lambda b, n: (jnp.minimum(n, 1), 0, 0)),
        ],
        out_specs=pl.BlockSpec((1, CHUNK, qw), lambda b, n: (b, n, 0)),
        out_shape=jax.ShapeDtypeStruct((batch, seq, qw), BF16),
        compiler_params=_compiler_params(
            ("parallel", "arbitrary"),
            pipelined=[((CHUNK, qw), BF16)] * 2 + [((CHUNK, kw), BF16)] * 4 + [((seq, LANES), F32)] * 2
                      + [((CHUNK, 2 * CHUNK), F32)],
            resident=[((SWA_Q_PER_KV * CHUNK, 2 * CHUNK), F32)] * 6),
        name="swa_attention",
    )(sinks, qkv, qkv, qkv, qkv, qkv, cos, sin, rot, bias).reshape(batch * seq, qw)


FOX_CUMSUM_CHUNK = 256
FOX_Q_SCALE = FOX_HEAD_DIM ** -0.5 * math.log2(math.e)


def _fox_decay_kernel(fl_ref, bf_ref, o_ref, *, n_chunks):
    cc = FOX_CUMSUM_CHUNK
    row = lax.broadcasted_iota(jnp.int32, (cc, cc), 0)
    col = lax.broadcasted_iota(jnp.int32, (cc, cc), 1)
    lower = jnp.where(col <= row, 1.0, 0.0).astype(BF16)
    carry = jnp.zeros((1, LANES), F32)
    for c in range(n_chunks):
        rows = slice(c * cc, (c + 1) * cc)
        x = jax.nn.log_sigmoid(fl_ref[0, rows, :] + bf_ref[...])
        hi = x.astype(BF16)
        rest = x - hi.astype(F32)
        mid = rest.astype(BF16)
        lo = (rest - mid.astype(F32)).astype(BF16)
        cs = (jnp.dot(lower, hi, preferred_element_type=F32) + jnp.dot(lower, mid, preferred_element_type=F32)
              + jnp.dot(lower, lo, preferred_element_type=F32)) + carry
        o_ref[0, rows, :] = cs
        carry = cs[cc - 1:cc, :]


def _fox_decay(fl, b_f, batch, seq):
    assert seq % FOX_CUMSUM_CHUNK == 0
    bf = jnp.zeros((1, LANES), F32).at[0, :FOX_HEADS].set(b_f)
    return pl.pallas_call(
        functools.partial(_fox_decay_kernel, n_chunks=seq // FOX_CUMSUM_CHUNK),
        grid=(batch,),
        in_specs=[pl.BlockSpec((1, seq, LANES), lambda b: (b, 0, 0)), pl.BlockSpec((1, LANES), lambda b: (0, 0))],
        out_specs=pl.BlockSpec((1, seq, LANES), lambda b: (b, 0, 0)),
        out_shape=jax.ShapeDtypeStruct((batch, seq, LANES), F32),
        compiler_params=_compiler_params(
            ("parallel",), pipelined=[((seq, LANES), F32)] * 2, resident=[((seq, LANES), F32)]),
        name="fox_decay",
    )(fl.reshape(batch, seq, LANES), bf)


def _fox_kernel(q_ref, k_ref, v_ref, dq_ref, dk_ref, o_ref, *, tq, n_tiles):
    head = pl.program_id(1)
    log2e = np.float32(math.log2(math.e))
    head_lane = lax.broadcasted_iota(jnp.int32, (1, LANES), 1) == head
    row = lax.broadcasted_iota(jnp.int32, (tq, tq), 0)
    col = lax.broadcasted_iota(jnp.int32, (tq, tq), 1)
    causal = col <= row

    for i in range(n_tiles):
        rows = slice(i * tq, (i + 1) * tq)
        dq = jnp.sum(jnp.where(head_lane, dq_ref[0, rows, :], 0.0), axis=1, keepdims=True) * log2e
        q = q_ref[0, rows, :]
        m = l = acc = None
        for kt in range(i + 1):
            keys = slice(kt * tq, (kt + 1) * tq)
            s = lax.dot_general(q, k_ref[0, keys, :], (((1,), (1,)), ((), ())), preferred_element_type=F32)
            s = s + dq - dk_ref[0, 0, kt:kt + 1, :] * log2e
            if kt == i:
                s = jnp.where(causal, s, NEG_INF)
            m_tile = jnp.max(s, axis=1, keepdims=True)
            m_new = m_tile if m is None else jnp.maximum(m, m_tile)
            p = jnp.exp2(s - m_new)
            l_tile = jnp.sum(p, axis=1, keepdims=True)
            pv = jnp.dot(p.astype(BF16), v_ref[0, keys, :], preferred_element_type=F32)
            if m is None:
                l, acc = l_tile, pv
            else:
                alpha = jnp.exp2(m - m_new)
                l, acc = alpha * l + l_tile, alpha * acc + pv
            m = m_new
        o_ref[0, rows, :] = (acc / l).astype(o_ref.dtype)


def _fox_attention(qkv, dec, batch, seq, *, tq=512):
    width = FOX_HEADS * FOX_HEAD_DIM
    tq = min(tq, seq)
    assert seq % tq == 0
    qkv = qkv.reshape(batch, seq, 3 * width)
    dec_k = dec[:, :, :FOX_HEADS].transpose(0, 2, 1).reshape(batch, FOX_HEADS, seq // tq, tq)

    def head_spec(first_block):
        return pl.BlockSpec((1, seq, FOX_HEAD_DIM), lambda b, h: (b, 0, first_block + h))

    return pl.pallas_call(
        functools.partial(_fox_kernel, tq=tq, n_tiles=seq // tq),
        grid=(batch, FOX_HEADS),
        in_specs=[
            head_spec(0), head_spec(FOX_HEADS), head_spec(2 * FOX_HEADS),
            pl.BlockSpec((1, seq, LANES), lambda b, h: (b, 0, 0)),
            pl.BlockSpec((1, 1, seq // tq, tq), lambda b, h: (b, h, 0, 0)),
        ],
        out_specs=head_spec(0),
        out_shape=jax.ShapeDtypeStruct((batch, seq, width), BF16),
        compiler_params=_compiler_params(
            ("parallel", "parallel"),
            pipelined=[((seq, FOX_HEAD_DIM), BF16)] * 4 + [((seq, LANES), F32), ((seq // tq, tq), F32)],
            resident=[((tq, tq), F32)] * 12),
        name="fox_attention",
    )(qkv, qkv, qkv, dec, dec_k).reshape(batch * seq, width)


def _gmlp_layer(h, w_in, v_gain, ws, bs, w_out):
    z = _norm_matmul(h, w_in, gelu=True)
    return _gmlp_mix(z, v_gain, ws, bs, w_out, h)


def _swa_layer(h, w_in, sinks, w_out, batch, seq):
    d = h.shape[1]
    qw = SWA_Q_HEADS * SWA_HEAD_DIM
    kvw = SWA_KV_HEADS * SWA_HEAD_DIM

    def duplicate_heads(w):
        w = w.reshape(d, SWA_KV_HEADS, 1, SWA_HEAD_DIM)
        return jnp.broadcast_to(w, (d, SWA_KV_HEADS, 2, SWA_HEAD_DIM)).reshape(d, SWA_KV_HEADS * LANES)

    w = jnp.concatenate([w_in[:, :qw], duplicate_heads(w_in[:, qw:qw + kvw]),
                         duplicate_heads(w_in[:, qw + kvw:])], axis=1)
    qkv = _norm_matmul(h, w)
    o = _swa_attention(qkv, sinks, batch, seq)
    return _matmul_residual(o, w_out, h)


def _fox_layer(h, norm, w_in, b_f, w_out, batch, seq):
    d = h.shape[1]
    width = FOX_HEADS * FOX_HEAD_DIM
    w_in = (norm[:, None] * w_in).astype(BF16)
    w_f = jnp.zeros((d, LANES), BF16).at[:, :FOX_HEADS].set(w_in[:, 3 * width:])
    qkv, fl = _norm_matmul(h, w_in[:, :3 * width], w_f, scaled_cols=width, scale=FOX_Q_SCALE)
    dec = _fox_decay(fl, b_f, batch, seq)
    o = _fox_attention(qkv, dec, batch, seq)
    return _matmul_residual(o, w_out, h)


def kernel(x, l0_ffn1_norm, l0_ffn1_wi, l0_ffn1_wo, l0_mix_norm, l0_mix_win, l0_gmlp_vnorm, l0_gmlp_ws, l0_gmlp_bs, l0_mix_wout, l0_ffn2_norm, l0_ffn2_wi, l0_ffn2_wo, l1_ffn1_norm, l1_ffn1_wi, l1_ffn1_wo, l1_mix_norm, l1_mix_win, l1_swa_sinks, l1_mix_wout, l1_ffn2_norm, l1_ffn2_wi, l1_ffn2_wo, l2_ffn1_norm, l2_ffn1_wi, l2_ffn1_wo, l2_mix_norm, l2_mix_win, l2_fox_bf, l2_mix_wout, l2_ffn2_norm, l2_ffn2_wi, l2_ffn2_wo, l3_ffn1_norm, l3_ffn1_wi, l3_ffn1_wo, l3_mix_norm, l3_mix_win, l3_gmlp_vnorm, l3_gmlp_ws, l3_gmlp_bs, l3_mix_wout, l3_ffn2_norm, l3_ffn2_wi, l3_ffn2_wo, final_norm):
    batch, seq, d = x.shape
    h = x.reshape(batch * seq, d)

    ffn_weights = [(l0_ffn1_norm, l0_ffn1_wi, l0_ffn1_wo), (l0_ffn2_norm, l0_ffn2_wi, l0_ffn2_wo),
                   (l1_ffn1_norm, l1_ffn1_wi, l1_ffn1_wo), (l1_ffn2_norm, l1_ffn2_wi, l1_ffn2_wo),
                   (l2_ffn1_norm, l2_ffn1_wi, l2_ffn1_wo), (l2_ffn2_norm, l2_ffn2_wi, l2_ffn2_wo),
                   (l3_ffn1_norm, l3_ffn1_wi, l3_ffn1_wo), (l3_ffn2_norm, l3_ffn2_wi, l3_ffn2_wo)]
    gain0, wi0, wo0 = ffn_weights[0]
    cast = [((gain0[:, None] * wi0).astype(BF16), (0.5 * wo0).astype(BF16))]

    def ffn(h, final_gain=None, cast_extra=()):
        index = len(cast) - 1
        wi, wo_half = cast[index]
        next_weights = ffn_weights[index + 1] if index + 1 < len(ffn_weights) else None
        h, cast_next, extra = _ffn(h, wi, wo_half, final_gain, next_weights, cast_extra)
        cast.append(cast_next)
        return h, extra

    h, (w_in, w_out) = ffn(h, cast_extra=((l0_mix_win, l0_mix_norm), (l0_mix_wout, None)))
    h = _gmlp_layer(h, w_in, l0_gmlp_vnorm, l0_gmlp_ws, l0_gmlp_bs, w_out)
    h, _ = ffn(h)

    h, (w_in, w_out) = ffn(h, cast_extra=((l1_mix_win, l1_mix_norm), (l1_mix_wout, None)))
    h = _swa_layer(h, w_in, l1_swa_sinks, w_out, batch, seq)
    h, _ = ffn(h)

    h, (w_out,) = ffn(h, cast_extra=((l2_mix_wout, None),))
    h = _fox_layer(h, l2_mix_norm, l2_mix_win, l2_fox_bf, w_out, batch, seq)
    h, _ = ffn(h)

    h, (w_in, w_out) = ffn(h, cast_extra=((l3_mix_win, l3_mix_norm), (l3_mix_wout, None)))
    h = _gmlp_layer(h, w_in, l3_gmlp_vnorm, l3_gmlp_ws, l3_gmlp_bs, w_out)
    h, _ = ffn(h, final_norm)
    return h.reshape(batch, seq, d)
```

```python
import functools
import math

import jax
import jax.numpy as jnp
import numpy as np
from jax import lax
from jax.experimental import pallas as pl
from jax.experimental.pallas import tpu as pltpu

F32 = jnp.float32
BF16 = jnp.bfloat16

NORM_EPS = 1e-5
NEG_INF = -1e30
CHUNK = 128
GMLP_GROUPS = 16
SWA_HEAD_DIM = 64
SWA_Q_HEADS = 32
SWA_KV_HEADS = 4
SWA_Q_PER_KV = SWA_Q_HEADS // SWA_KV_HEADS
ROPE_THETA = 500000.0
ROPE_DIM = SWA_HEAD_DIM // 4
FOX_HEAD_DIM = 128
FOX_HEADS = 16

LANES = 128
V7X_VMEM_BYTES = 64 * 1024 * 1024
V7X_VMEM_USABLE_BYTES = 60 * 1024 * 1024


def _nbytes(shape, dtype):
    return math.prod(shape) * jnp.dtype(dtype).itemsize


def _compiler_params(semantics, pipelined, resident):
    need = 2 * sum(_nbytes(s, d) for s, d in pipelined) + sum(_nbytes(s, d) for s, d in resident)
    limit = min(V7X_VMEM_USABLE_BYTES, max(32 * 1024 * 1024, need + need // 4))
    return pltpu.CompilerParams(dimension_semantics=semantics, vmem_limit_bytes=limit)


def _rms(x, gain):
    return x * lax.rsqrt(jnp.mean(x * x, axis=-1, keepdims=True) + NORM_EPS) * gain


def _rms_prepare(x, rinv_ref, xb_ref):
    rinv = lax.rsqrt(jnp.mean(x * x, axis=-1, keepdims=True) + NORM_EPS)
    rinv_ref[...] = jnp.broadcast_to(rinv, rinv_ref.shape)
    xb_ref[...] = x.astype(BF16)


def _row_scale(y, rinv):
    return y * jnp.concatenate([rinv] * (y.shape[1] // LANES), axis=1)


def _ffn_kernel(*refs, n_ff_tiles, final_norm, cast_next, cast_extra_gains):
    refs = list(refs)
    x_ref, wg_ref, wu_ref, wo_ref = refs[:4]
    del refs[:4]
    gf_ref = refs.pop(0) if final_norm else None
    if cast_next:
        wi_next_ref, wi_next_gain_ref, wo_next_ref = refs[:3]
        del refs[:3]
    extra_refs = []
    for has_gain in cast_extra_gains:
        extra_refs.append((refs.pop(0), refs.pop(0) if has_gain else None))
    o_ref = refs.pop(0)
    if cast_next:
        wi_next_bf16_ref, wo_next_bf16_ref = refs[:2]
        del refs[:2]
    extra_bf16_refs = refs[:len(cast_extra_gains)]
    del refs[:len(cast_extra_gains)]
    xb_ref, rinv_ref = refs
    j = pl.program_id(1)

    @pl.when(j == 0)
    def _():
        x = x_ref[...]
        _rms_prepare(x, rinv_ref, xb_ref)
        o_ref[...] = x

    xb = xb_ref[...]
    rinv = rinv_ref[...]
    gate = _row_scale(jnp.dot(xb, wg_ref[...], preferred_element_type=F32), rinv)
    up = _row_scale(jnp.dot(xb, wu_ref[...], preferred_element_type=F32), rinv)
    act = (gate * jax.nn.sigmoid(gate) * up).astype(BF16)
    o_ref[...] += jnp.dot(act, wo_ref[...], preferred_element_type=F32)

    if cast_next:
        wi_next_bf16_ref[...] = (wi_next_gain_ref[...] * wi_next_ref[...]).astype(BF16)
        wo_next_bf16_ref[...] = (0.5 * wo_next_ref[...]).astype(BF16)
    for (src_ref, gain_ref), dst_ref in zip(extra_refs, extra_bf16_refs):
        w = src_ref[...]
        dst_ref[...] = (w if gain_ref is None else gain_ref[...] * w).astype(BF16)

    if final_norm:
        @pl.when(j == n_ff_tiles - 1)
        def _():
            o_ref[...] = _rms(o_ref[...], gf_ref[...])


BF16_SUBLANES = 16


def _ffn(h, wi, wo_half, final_gain=None, next_weights=None, cast_extra=(), *, tm=1024, tf=512):
    wo = wo_half
    t, d = h.shape
    tm = min(tm, t)
    d_ff = wo.shape[0]
    n_ff_tiles = d_ff // tf
    n_row_tiles = t // tm
    assert t % tm == 0 and d_ff % tf == 0
    in_specs = [
        pl.BlockSpec((tm, d), lambda i, j: (i, 0)),
        pl.BlockSpec((d, tf), lambda i, j: (0, j)),
        pl.BlockSpec((d, tf), lambda i, j: (0, j + n_ff_tiles)),
        pl.BlockSpec((tf, d), lambda i, j: (j, 0)),
    ]
    args = [h, wi, wi, wo]
    out_specs = [pl.BlockSpec((tm, d), lambda i, j: (i, 0))]
    out_shape = [jax.ShapeDtypeStruct((t, d), F32)]
    pipelined = [((tm, d), F32), ((tm, d), F32), ((d, tf), BF16), ((d, tf), BF16), ((tf, d), BF16)]
    if final_gain is not None:
        in_specs.append(pl.BlockSpec((1, d), lambda i, j: (0, 0)))
        args.append(final_gain.reshape(1, d))
    if next_weights is not None:
        gain_next, wi_next, wo_next = next_weights
        assert wi_next.shape == wi.shape and wo_next.shape == wo.shape
        assert d % n_row_tiles == 0 and wi.shape[1] % n_ff_tiles == 0
        wi_block = (d // n_row_tiles, wi.shape[1] // n_ff_tiles)
        wo_block = (d_ff // n_ff_tiles, d // n_row_tiles)
        wi_spec = pl.BlockSpec(wi_block, lambda i, j: (i, j))
        wo_spec = pl.BlockSpec(wo_block, lambda i, j: (j, i))
        in_specs += [wi_spec, pl.BlockSpec((wi_block[0], 1), lambda i, j: (i, 0)), wo_spec]
        args += [wi_next, gain_next.reshape(d, 1), wo_next]
        out_specs += [wi_spec, wo_spec]
        out_shape += [jax.ShapeDtypeStruct(wi.shape, BF16), jax.ShapeDtypeStruct(wo.shape, BF16)]
        pipelined += [(wi_block, F32), (wo_block, F32), (wi_block, BF16), (wo_block, BF16), ((wi_block[0], LANES), F32)]
    extra_out_specs, extra_out_shape = [], []
    for w, row_gain in cast_extra:
        rows, cols = w.shape
        n_sub = max(s for s in range(1, n_ff_tiles + 1)
                    if rows % (n_row_tiles * s * BF16_SUBLANES) == 0)
        block_rows = rows // (n_row_tiles * n_sub)

        def row_block(i, j, n_sub=n_sub):
            return (i * n_sub + jnp.minimum(j, n_sub - 1), 0)

        spec = pl.BlockSpec((block_rows, cols), row_block)
        in_specs.append(spec)
        args.append(w)
        if row_gain is not None:
            in_specs.append(pl.BlockSpec((block_rows, 1), row_block))
            args.append(row_gain.reshape(rows, 1))
            pipelined.append(((block_rows, LANES), F32))
        extra_out_specs.append(spec)
        extra_out_shape.append(jax.ShapeDtypeStruct(w.shape, BF16))
        pipelined += [((block_rows, cols), F32), ((block_rows, cols), BF16)]
    out_specs += extra_out_specs
    out_shape += extra_out_shape
    outs = pl.pallas_call(
        functools.partial(_ffn_kernel, n_ff_tiles=n_ff_tiles, final_norm=final_gain is not None,
                          cast_next=next_weights is not None,
                          cast_extra_gains=tuple(g is not None for _, g in cast_extra)),
        grid=(n_row_tiles, n_ff_tiles),
        in_specs=in_specs,
        out_specs=out_specs,
        out_shape=out_shape,
        scratch_shapes=[pltpu.VMEM((tm, d), BF16), pltpu.VMEM((tm, LANES), F32)],
        compiler_params=_compiler_params(
            ("parallel", "arbitrary"), pipelined=pipelined,
            resident=[((tm, d), BF16), ((tm, LANES), F32), ((tm, tf), F32), ((tm, tf), F32), ((tm, tf), F32)]),
        name="ffn",
    )(*args)
    n_next = 0 if next_weights is None else 2
    return outs[0], (tuple(outs[1:3]) if n_next else None), tuple(outs[1 + n_next:])


def _norm_matmul_kernel(*refs, gelu, side, scaled_tiles, scale):
    if side:
        x_ref, w_ref, w2_ref, o_ref, o2_ref, xb_ref, rinv_ref = refs
    else:
        x_ref, w_ref, o_ref, xb_ref, rinv_ref = refs
    j = pl.program_id(1)

    @pl.when(j == 0)
    def _():
        _rms_prepare(x_ref[...], rinv_ref, xb_ref)
        if side:
            o2_ref[...] = _row_scale(jnp.dot(xb_ref[...], w2_ref[...], preferred_element_type=F32), rinv_ref[...])

    y = _row_scale(jnp.dot(xb_ref[...], w_ref[...], preferred_element_type=F32), rinv_ref[...])
    if gelu:
        y = 0.5 * y * (1.0 + lax.erf(y * np.float32(math.sqrt(0.5))))
    if scaled_tiles:
        y = y * jnp.where(j < scaled_tiles, np.float32(scale), np.float32(1.0))
    o_ref[...] = y.astype(o_ref.dtype)


def _norm_matmul(h, w, w_side=None, *, n_cols=None, gelu=False, scaled_cols=0, scale=1.0, tm=1024, tn=1024):
    t, d = h.shape
    tm = min(tm, t)
    n = w.shape[1] if n_cols is None else n_cols
    assert t % tm == 0 and n % tn == 0 and n <= w.shape[1] and scaled_cols % tn == 0
    side = w_side is not None
    in_specs = [
        pl.BlockSpec((tm, d), lambda i, j: (i, 0)),
        pl.BlockSpec((d, tn), lambda i, j: (0, j)),
    ]
    args = [h, w]
    out_specs = pl.BlockSpec((tm, tn), lambda i, j: (i, j))
    out_shape = jax.ShapeDtypeStruct((t, n), BF16)
    pipelined = [((tm, d), F32), ((d, tn), BF16), ((tm, tn), BF16)]
    if side:
        n2 = w_side.shape[1]
        assert n2 == LANES
        in_specs.append(pl.BlockSpec((d, n2), lambda i, j: (0, 0)))
        args.append(w_side)
        out_specs = [out_specs, pl.BlockSpec((tm, n2), lambda i, j: (i, 0))]
        out_shape = [out_shape, jax.ShapeDtypeStruct((t, n2), F32)]
        pipelined += [((d, n2), BF16), ((tm, n2), F32)]
    return pl.pallas_call(
        functools.partial(_norm_matmul_kernel, gelu=gelu, side=side, scaled_tiles=scaled_cols // tn, scale=scale),
        grid=(t // tm, n // tn),
        in_specs=in_specs,
        out_specs=out_specs,
        out_shape=out_shape,
        scratch_shapes=[pltpu.VMEM((tm, d), BF16), pltpu.VMEM((tm, LANES), F32)],
        compiler_params=_compiler_params(
            ("parallel", "arbitrary"), pipelined=pipelined,
            resident=[((tm, d), BF16), ((tm, LANES), F32), ((tm, tn), F32), ((tm, tn), F32)]),
        name="norm_matmul",
    )(*args)


def _matmul_residual_kernel(y_ref, w_ref, h_ref, o_ref):
    o_ref[...] = h_ref[...] + jnp.dot(y_ref[...], w_ref[...], preferred_element_type=F32)


def _matmul_residual(y, w, h, *, tm=512):
    t, k = y.shape
    d = w.shape[1]
    assert t % tm == 0
    return pl.pallas_call(
        _matmul_residual_kernel,
        grid=(t // tm,),
        in_specs=[
            pl.BlockSpec((tm, k), lambda i: (i, 0)),
            pl.BlockSpec((k, d), lambda i: (0, 0)),
            pl.BlockSpec((tm, d), lambda i: (i, 0)),
        ],
        out_specs=pl.BlockSpec((tm, d), lambda i: (i, 0)),
        out_shape=jax.ShapeDtypeStruct((t, d), F32),
        compiler_params=_compiler_params(
            ("parallel",),
            pipelined=[((tm, k), BF16), ((k, d), BF16), ((tm, d), F32), ((tm, d), F32)],
            resident=[((tm, d), F32)]),
        name="matmul_residual",
    )(y, w, h)


def _gmlp_mix_kernel(u_ref, v_ref, vg_ref, ws_ref, bias_ref, wout_ref, h_ref, o_ref, wsm_ref, y_ref,
                     *, n_chunks):
    @pl.when(pl.program_id(0) == 0)
    def _():
        row = lax.broadcasted_iota(jnp.int32, (CHUNK, CHUNK), 0)
        col = lax.broadcasted_iota(jnp.int32, (CHUNK, CHUNK), 1)
        for g in range(GMLP_GROUPS):
            wsm_ref[g] = jnp.where(col <= row, ws_ref[g], 0.0).astype(BF16)

    chunks_per_part = max(n_chunks // 2, 1)
    for c in range(n_chunks):
        rows = slice(c * CHUNK, (c + 1) * CHUNK)
        vn = _rms(v_ref[rows, :].astype(F32), vg_ref[...]).astype(BF16)
        for g in range(GMLP_GROUPS):
            cols = slice(g * LANES, (g + 1) * LANES)
            mixed = jnp.dot(wsm_ref[g], vn[:, cols], preferred_element_type=F32) + bias_ref[:, cols]
            y_ref[rows, cols] = (u_ref[rows, cols].astype(F32) * mixed).astype(BF16)
        if (c + 1) % chunks_per_part == 0:
            part = slice((c + 1 - chunks_per_part) * CHUNK, (c + 1) * CHUNK)
            o_ref[part, :] = h_ref[part, :] + jnp.dot(y_ref[part, :], wout_ref[...], preferred_element_type=F32)


def _gmlp_mix(z, v_gain, ws, bs, w_out, h, *, tm=512):
    t, d = h.shape
    tm = min(tm, t)
    width = z.shape[1] // 2
    assert width == GMLP_GROUPS * LANES and t % tm == 0 and tm % CHUNK == 0
    n_width_blocks = 1
    bias = jnp.repeat(bs.T, LANES, axis=1)
    return pl.pallas_call(
        functools.partial(_gmlp_mix_kernel, n_chunks=tm // CHUNK),
        grid=(t // tm,),
        in_specs=[
            pl.BlockSpec((tm, width), lambda i: (i, 0)),
            pl.BlockSpec((tm, width), lambda i: (i, n_width_blocks)),
            pl.BlockSpec((1, width), lambda i: (0, 0)),
            pl.BlockSpec((GMLP_GROUPS, CHUNK, CHUNK), lambda i: (0, 0, 0)),
            pl.BlockSpec((CHUNK, width), lambda i: (0, 0)),
            pl.BlockSpec((width, d), lambda i: (0, 0)),
            pl.BlockSpec((tm, d), lambda i: (i, 0)),
        ],
        out_specs=pl.BlockSpec((tm, d), lambda i: (i, 0)),
        out_shape=jax.ShapeDtypeStruct((t, d), F32),
        scratch_shapes=[pltpu.VMEM((GMLP_GROUPS, CHUNK, CHUNK), BF16), pltpu.VMEM((tm, width), BF16)],
        compiler_params=_compiler_params(
            ("arbitrary",),
            pipelined=[((tm, width), BF16), ((tm, width), BF16), ((1, width), F32),
                       ((GMLP_GROUPS, CHUNK, CHUNK), F32), ((CHUNK, width), F32), ((width, d), BF16),
                       ((tm, d), F32), ((tm, d), F32)],
            resident=[((GMLP_GROUPS, CHUNK, CHUNK), BF16), ((tm, width), BF16), ((tm, d), F32)]),
        name="gmlp_mix",
    )(z, z, v_gain.reshape(1, width), ws, bias, w_out, h)


def _rope_tables(seq):
    half = ROPE_DIM // 2
    inv_freq = ROPE_THETA ** (-(jnp.arange(half, dtype=F32) * 2.0 / ROPE_DIM))
    ang = jnp.arange(seq, dtype=F32)[:, None] * inv_freq[None, :]
    cos, sin = jnp.cos(ang), jnp.sin(ang)
    rest = jnp.zeros((seq, SWA_HEAD_DIM - ROPE_DIM), F32)
    cos_h = jnp.concatenate([cos, cos, rest + 1.0], axis=1)
    sin_h = jnp.concatenate([sin, sin, rest], axis=1)
    rot = np.zeros((LANES, LANES), np.float32)
    for i in range(LANES):
        if i % SWA_HEAD_DIM < half:
            rot[i + half, i] = -1.0
        elif i % SWA_HEAD_DIM < ROPE_DIM:
            rot[i - half, i] = 1.0
    return (jnp.concatenate([cos_h, cos_h], axis=1), jnp.concatenate([sin_h, sin_h], axis=1),
            jnp.asarray(rot, BF16))


def _swa_kernel(sinks_ref, q_ref, kc_ref, kp_ref, vc_ref, vp_ref, cos_ref, sin_ref, rot_ref, bias_ref, o_ref):
    n = pl.program_id(1)
    log2e = np.float32(math.log2(math.e))
    cur = pl.ds(pl.multiple_of(n * CHUNK, CHUNK), CHUNK)
    prev = pl.ds(pl.multiple_of(jnp.maximum(n - 1, 0) * CHUNK, CHUNK), CHUNK)
    low_lanes = lax.broadcasted_iota(jnp.int32, (CHUNK, LANES), 1) < SWA_HEAD_DIM
    pairs = SWA_Q_PER_KV // 2
    q_scale = np.float32(SWA_HEAD_DIM ** -0.5 * math.log2(math.e))

    n_q_blocks = SWA_KV_HEADS * pairs
    pieces = [q_ref[0, :, b * LANES:(b + 1) * LANES] for b in range(n_q_blocks)]
    pieces += [kc_ref[0, :, g * LANES:(g + 1) * LANES] for g in range(SWA_KV_HEADS)]
    pieces += [kp_ref[0, :, g * LANES:(g + 1) * LANES] for g in range(SWA_KV_HEADS)]
    partner = jnp.dot(jnp.concatenate(pieces, axis=0), rot_ref[...], preferred_element_type=F32)

    def rope(idx, rows):
        return (pieces[idx].astype(F32) * cos_ref[rows, :]
                + partner[idx * CHUNK:(idx + 1) * CHUNK] * sin_ref[rows, :])

    scores = []
    for g in range(SWA_KV_HEADS):
        k = jnp.concatenate([rope(n_q_blocks + SWA_KV_HEADS + g, prev), rope(n_q_blocks + g, cur)],
                            axis=0).astype(BF16)
        stacked = []
        for pair in range(pairs):
            q = rope(g * pairs + pair, cur) * q_scale
            stacked += [jnp.where(low_lanes, q, 0.0), jnp.where(low_lanes, 0.0, q)]
        q8 = jnp.concatenate(stacked, axis=0).astype(BF16)
        scores.append(lax.dot_general(q8, k, (((1,), (1,)), ((), ())), preferred_element_type=F32))

    for g in range(SWA_KV_HEADS):
        lanes_g = slice(g * LANES, (g + 1) * LANES)
        v = jnp.concatenate([vp_ref[0, :, lanes_g], vc_ref[0, :, lanes_g]], axis=0)
        s = scores[g]
        probs, inv_denoms = [], []
        for h in range(SWA_Q_PER_KV):
            rows = slice(h * CHUNK, (h + 1) * CHUNK)
            s_h = s[rows] + bias_ref[0]
            sink = sinks_ref[g * SWA_Q_PER_KV + h] * log2e
            m = jnp.maximum(jnp.max(s_h, axis=1, keepdims=True), sink)
            p = jnp.exp2(s_h - m)
            inv_denoms.append(1.0 / (jnp.sum(p, axis=1, keepdims=True) + jnp.exp2(sink - m)))
            probs.append(p.astype(BF16))
        o8 = jnp.dot(jnp.concatenate(probs, axis=0), v, preferred_element_type=F32)
        for pair in range(pairs):
            lanes_q = slice((g * pairs + pair) * LANES, (g * pairs + pair + 1) * LANES)
            lo = o8[2 * pair * CHUNK:(2 * pair + 1) * CHUNK] * inv_denoms[2 * pair]
            hi = o8[(2 * pair + 1) * CHUNK:(2 * pair + 2) * CHUNK] * inv_denoms[2 * pair + 1]
            o_ref[0, :, lanes_q] = jnp.where(low_lanes, lo, hi).astype(o_ref.dtype)


def _swa_attention(qkv, sinks, batch, seq):
    nb = seq // CHUNK
    qw = SWA_Q_HEADS * SWA_HEAD_DIM
    kw = SWA_KV_HEADS * LANES
    qkv = qkv.reshape(batch, seq, qw + 2 * kw)
    cos, sin, rot = _rope_tables(seq)
    k_blk = qw // kw
    table_spec = pl.BlockSpec((seq, LANES), lambda b, n: (0, 0))
    r = np.arange(CHUNK)[:, None]
    c = np.arange(2 * CHUNK)[None, :]
    band = (c > r) & (c <= r + CHUNK)
    bias = jnp.asarray(np.where(np.stack([band & (c >= CHUNK), band]), 0.0, NEG_INF), F32)
    return pl.pallas_call(
        _swa_kernel,
        grid=(batch, nb),
        in_specs=[
            pl.BlockSpec(memory_space=pltpu.SMEM),
            pl.BlockSpec((1, CHUNK, qw), lambda b, n: (b, n, 0)),
            pl.BlockSpec((1, CHUNK, kw), lambda b, n: (b, n, k_blk)),
            pl.BlockSpec((1, CHUNK, kw), lambda b, n: (b, jnp.maximum(n - 1, 0), k_blk)),
            pl.BlockSpec((1, CHUNK, kw), lambda b, n: (b, n, k_blk + 1)),
            pl.BlockSpec((1, CHUNK, kw), lambda b, n: (b, jnp.maximum(n - 1, 0), k_blk + 1)),
            table_spec, table_spec, pl.BlockSpec((LANES, LANES), lambda b, n: (0, 0)),
            pl.BlockSpec((1, CHUNK, 2 * CHUNK), lambda b, n: (jnp.minimum(n, 1), 0, 0)),
        ],
        out_specs=pl.BlockSpec((1, CHUNK, qw), lambda b, n: (b, n, 0)),
        out_shape=jax.ShapeDtypeStruct((batch, seq, qw), BF16),
        compiler_params=_compiler_params(
            ("parallel", "arbitrary"),
            pipelined=[((CHUNK, qw), BF16)] * 2 + [((CHUNK, kw), BF16)] * 4 + [((seq, LANES), F32)] * 2
                      + [((CHUNK, 2 * CHUNK), F32)],
            resident=[((SWA_Q_PER_KV * CHUNK, 2 * CHUNK), F32)] * 6),
        name="swa_attention",
    )(sinks, qkv, qkv, qkv, qkv, qkv, cos, sin, rot, bias).reshape(batch * seq, qw)


FOX_CUMSUM_CHUNK = 256
FOX_Q_SCALE = FOX_HEAD_DIM ** -0.5 * math.log2(math.e)


def _fox_decay_kernel(fl_ref, bf_ref, o_ref, *, n_chunks):
    cc = FOX_CUMSUM_CHUNK
    row = lax.broadcasted_iota(jnp.int32, (cc, cc), 0)
    col = lax.broadcasted_iota(jnp.int32, (cc, cc), 1)
    lower = jnp.where(col <= row, 1.0, 0.0).astype(BF16)
    carry = jnp.zeros((1, LANES), F32)
    for c in range(n_chunks):
        rows = slice(c * cc, (c + 1) * cc)
        x = jax.nn.log_sigmoid(fl_ref[0, rows, :] + bf_ref[...])
        hi = x.astype(BF16)
        rest = x - hi.astype(F32)
        mid = rest.astype(BF16)
        lo = (rest - mid.astype(F32)).astype(BF16)
        cs = (jnp.dot(lower, hi, preferred_element_type=F32) + jnp.dot(lower, mid, preferred_element_type=F32)
              + jnp.dot(lower, lo, preferred_element_type=F32)) + carry
        o_ref[0, rows, :] = cs
        carry = cs[cc - 1:cc, :]


def _fox_decay(fl, b_f, batch, seq):
    assert seq % FOX_CUMSUM_CHUNK == 0
    bf = jnp.zeros((1, LANES), F32).at[0, :FOX_HEADS].set(b_f)
    return pl.pallas_call(
        functools.partial(_fox_decay_kernel, n_chunks=seq // FOX_CUMSUM_CHUNK),
        grid=(batch,),
        in_specs=[pl.BlockSpec((1, seq, LANES), lambda b: (b, 0, 0)), pl.BlockSpec((1, LANES), lambda b: (0, 0))],
        out_specs=pl.BlockSpec((1, seq, LANES), lambda b: (b, 0, 0)),
        out_shape=jax.ShapeDtypeStruct((batch, seq, LANES), F32),
        compiler_params=_compiler_params(
            ("parallel",), pipelined=[((seq, LANES), F32)] * 2, resident=[((seq, LANES), F32)]),
        name="fox_decay",
    )(fl.reshape(batch, seq, LANES), bf)


def _fox_kernel(q_ref, k_ref, v_ref, dq_ref, dk_ref, o_ref, *, tq, n_tiles):
    head = pl.program_id(1)
    log2e = np.float32(math.log2(math.e))
    head_lane = lax.broadcasted_iota(jnp.int32, (1, LANES), 1) == head
    row = lax.broadcasted_iota(jnp.int32, (tq, tq), 0)
    col = lax.broadcasted_iota(jnp.int32, (tq, tq), 1)
    causal = col <= row

    for i in range(n_tiles):
        rows = slice(i * tq, (i + 1) * tq)
        dq = jnp.sum(jnp.where(head_lane, dq_ref[0, rows, :], 0.0), axis=1, keepdims=True) * log2e
        q = q_ref[0, rows, :]
        m = l = acc = None
        for kt in range(i + 1):
            keys = slice(kt * tq, (kt + 1) * tq)
            s = lax.dot_general(q, k_ref[0, keys, :], (((1,), (1,)), ((), ())), preferred_element_type=F32)
            s = s + dq - dk_ref[0, 0, kt:kt + 1, :] * log2e
            if kt == i:
                s = jnp.where(causal, s, NEG_INF)
            m_tile = jnp.max(s, axis=1, keepdims=True)
            m_new = m_tile if m is None else jnp.maximum(m, m_tile)
            p = jnp.exp2(s - m_new)
            l_tile = jnp.sum(p, axis=1, keepdims=True)
            pv = jnp.dot(p.astype(BF16), v_ref[0, keys, :], preferred_element_type=F32)
            if m is None:
                l, acc = l_tile, pv
            else:
                alpha = jnp.exp2(m - m_new)
                l, acc = alpha * l + l_tile, alpha * acc + pv
            m = m_new
        o_ref[0, rows, :] = (acc / l).astype(o_ref.dtype)


def _fox_attention(qkv, dec, batch, seq, *, tq=512):
    width = FOX_HEADS * FOX_HEAD_DIM
    tq = min(tq, seq)
    assert seq % tq == 0
    qkv = qkv.reshape(batch, seq, 3 * width)
    dec_k = dec[:, :, :FOX_HEADS].transpose(0, 2, 1).reshape(batch, FOX_HEADS, seq // tq, tq)

    def head_spec(first_block):
        return pl.BlockSpec((1, seq, FOX_HEAD_DIM), lambda b, h: (b, 0, first_block + h))

    return pl.pallas_call(
        functools.partial(_fox_kernel, tq=tq, n_tiles=seq // tq),
        grid=(batch, FOX_HEADS),
        in_specs=[
            head_spec(0), head_spec(FOX_HEADS), head_spec(2 * FOX_HEADS),
            pl.BlockSpec((1, seq, LANES), lambda b, h: (b, 0, 0)),
            pl.BlockSpec((1, 1, seq // tq, tq), lambda b, h: (b, h, 0, 0)),
        ],
        out_specs=head_spec(0),
        out_shape=jax.ShapeDtypeStruct((batch, seq, width), BF16),
        compiler_params=_compiler_params(
            ("parallel", "parallel"),
            pipelined=[((seq, FOX_HEAD_DIM), BF16)] * 4 + [((seq, LANES), F32), ((seq // tq, tq), F32)],
            resident=[((tq, tq), F32)] * 12),
        name="fox_attention",
    )(qkv, qkv, qkv, dec, dec_k).reshape(batch * seq, width)


def _gmlp_layer(h, w_in, v_gain, ws, bs, w_out):
    z = _norm_matmul(h, w_in, gelu=True)
    return _gmlp_mix(z, v_gain, ws, bs, w_out, h)


def _swa_layer(h, w_in, sinks, w_out, batch, seq):
    d = h.shape[1]
    qw = SWA_Q_HEADS * SWA_HEAD_DIM
    kvw = SWA_KV_HEADS * SWA_HEAD_DIM

    def duplicate_heads(w):
        w = w.reshape(d, SWA_KV_HEADS, 1, SWA_HEAD_DIM)
        return jnp.broadcast_to(w, (d, SWA_KV_HEADS, 2, SWA_HEAD_DIM)).reshape(d, SWA_KV_HEADS * LANES)

    w = jnp.concatenate([w_in[:, :qw], duplicate_heads(w_in[:, qw:qw + kvw]),
                         duplicate_heads(w_in[:, qw + kvw:])], axis=1)
    qkv = _norm_matmul(h, w)
    o = _swa_attention(qkv, sinks, batch, seq)
    return _matmul_residual(o, w_out, h)


def _fox_layer(h, norm, w_in, b_f, w_out, batch, seq):
    d = h.shape[1]
    width = FOX_HEADS * FOX_HEAD_DIM
    w_in = (norm[:, None] * w_in).astype(BF16)
    w_f = jnp.zeros((d, LANES), BF16).at[:, :FOX_HEADS].set(w_in[:, 3 * width:])
    qkv, fl = _norm_matmul(h, w_in[:, :3 * width], w_f, scaled_cols=width, scale=FOX_Q_SCALE)
    dec = _fox_decay(fl, b_f, batch, seq)
    o = _fox_attention(qkv, dec, batch, seq)
    return _matmul_residual(o, w_out, h)


def kernel(x, l0_ffn1_norm, l0_ffn1_wi, l0_ffn1_wo, l0_mix_norm, l0_mix_win, l0_gmlp_vnorm, l0_gmlp_ws, l0_gmlp_bs, l0_mix_wout, l0_ffn2_norm, l0_ffn2_wi, l0_ffn2_wo, l1_ffn1_norm, l1_ffn1_wi, l1_ffn1_wo, l1_mix_norm, l1_mix_win, l1_swa_sinks, l1_mix_wout, l1_ffn2_norm, l1_ffn2_wi, l1_ffn2_wo, l2_ffn1_norm, l2_ffn1_wi, l2_ffn1_wo, l2_mix_norm, l2_mix_win, l2_fox_bf, l2_mix_wout, l2_ffn2_norm, l2_ffn2_wi, l2_ffn2_wo, l3_ffn1_norm, l3_ffn1_wi, l3_ffn1_wo, l3_mix_norm, l3_mix_win, l3_gmlp_vnorm, l3_gmlp_ws, l3_gmlp_bs, l3_mix_wout, l3_ffn2_norm, l3_ffn2_wi, l3_ffn2_wo, final_norm):
    batch, seq, d = x.shape
    h = x.reshape(batch * seq, d)

    ffn_weights = [(l0_ffn1_norm, l0_ffn1_wi, l0_ffn1_wo), (l0_ffn2_norm, l0_ffn2_wi, l0_ffn2_wo),
                   (l1_ffn1_norm, l1_ffn1_wi, l1_ffn1_wo), (l1_ffn2_norm, l1_ffn2_wi, l1_ffn2_wo),
                   (l2_ffn1_norm, l2_ffn1_wi, l2_ffn1_wo), (l2_ffn2_norm, l2_ffn2_wi, l2_ffn2_wo),
                   (l3_ffn1_norm, l3_ffn1_wi, l3_ffn1_wo), (l3_ffn2_norm, l3_ffn2_wi, l3_ffn2_wo)]
    gain0, wi0, wo0 = ffn_weights[0]
    cast = [((gain0[:, None] * wi0).astype(BF16), (0.5 * wo0).astype(BF16))]

    def ffn(h, final_gain=None, cast_extra=()):
        index = len(cast) - 1
        wi, wo_half = cast[index]
        next_weights = ffn_weights[index + 1] if index + 1 < len(ffn_weights) else None
        h, cast_next, extra = _ffn(h, wi, wo_half, final_gain, next_weights, cast_extra)
        cast.append(cast_next)
        return h, extra

    h, (w_in, w_out) = ffn(h, cast_extra=((l0_mix_win, l0_mix_norm), (l0_mix_wout, None)))
    h = _gmlp_layer(h, w_in, l0_gmlp_vnorm, l0_gmlp_ws, l0_gmlp_bs, w_out)
    h, _ = ffn(h)

    h, (w_in, w_out) = ffn(h, cast_extra=((l1_mix_win, l1_mix_norm), (l1_mix_wout, None)))
    h = _swa_layer(h, w_in, l1_swa_sinks, w_out, batch, seq)
    h, _ = ffn(h)

    h, (w_out,) = ffn(h, cast_extra=((l2_mix_wout, None),))
    h = _fox_layer(h, l2_mix_norm, l2_mix_win, l2_fox_bf, w_out, batch, seq)
    h, _ = ffn(h)

    h, (w_in, w_out) = ffn(h, cast_extra=((l3_mix_win, l3_mix_norm), (l3_mix_wout, None)))
    h = _gmlp_layer(h, w_in, l3_gmlp_vnorm, l3_gmlp_ws, l3_gmlp_bs, w_out)
    h, _ = ffn(h, final_norm)
    return h.reshape(batch, seq, d)
```

```python
import functools
import math

import jax
import jax.numpy as jnp
import numpy as np
from jax import lax
from jax.experimental import pallas as pl
from jax.experimental.pallas import tpu as pltpu

F32 = jnp.float32
BF16 = jnp.bfloat16

NORM_EPS = 1e-5
NEG_INF = -1e30
CHUNK = 128
GMLP_GROUPS = 16
SWA_HEAD_DIM = 64
SWA_Q_HEADS = 32
SWA_KV_HEADS = 4
SWA_Q_PER_KV = SWA_Q_HEADS // SWA_KV_HEADS
ROPE_THETA = 500000.0
ROPE_DIM = SWA_HEAD_DIM // 4
FOX_HEAD_DIM = 128
FOX_HEADS = 16

LANES = 128
V7X_VMEM_BYTES = 64 * 1024 * 1024
V7X_VMEM_USABLE_BYTES = 60 * 1024 * 1024


def _nbytes(shape, dtype):
    return math.prod(shape) * jnp.dtype(dtype).itemsize


def _compiler_params(semantics, pipelined, resident):
    need = 2 * sum(_nbytes(s, d) for s, d in pipelined) + sum(_nbytes(s, d) for s, d in resident)
    limit = min(V7X_VMEM_USABLE_BYTES, max(32 * 1024 * 1024, need + need // 4))
    return pltpu.CompilerParams(dimension_semantics=semantics, vmem_limit_bytes=limit)


def _rms(x, gain):
    return x * lax.rsqrt(jnp.mean(x * x, axis=-1, keepdims=True) + NORM_EPS) * gain


def _rms_prepare(x, rinv_ref, xb_ref):
    rinv = lax.rsqrt(jnp.mean(x * x, axis=-1, keepdims=True) + NORM_EPS)
    rinv_ref[...] = jnp.broadcast_to(rinv, rinv_ref.shape)
    xb_ref[...] = x.astype(BF16)


def _row_scale(y, rinv):
    return y * jnp.concatenate([rinv] * (y.shape[1] // LANES), axis=1)


def _ffn_kernel(*refs, n_ff_tiles, final_norm, cast_next, cast_extra_gains):
    refs = list(refs)
    x_ref, wg_ref, wu_ref, wo_ref = refs[:4]
    del refs[:4]
    gf_ref = refs.pop(0) if final_norm else None
    if cast_next:
        wi_next_ref, wi_next_gain_ref, wo_next_ref = refs[:3]
        del refs[:3]
    extra_refs = []
    for has_gain in cast_extra_gains:
        extra_refs.append((refs.pop(0), refs.pop(0) if has_gain else None))
    o_ref = refs.pop(0)
    if cast_next:
        wi_next_bf16_ref, wo_next_bf16_ref = refs[:2]
        del refs[:2]
    extra_bf16_refs = refs[:len(cast_extra_gains)]
    del refs[:len(cast_extra_gains)]
    xb_ref, rinv_ref = refs
    j = pl.program_id(1)

    def step(first):
        if first:
            _rms_prepare(x_ref[...], rinv_ref, xb_ref)
        xb = xb_ref[...]
        rinv = rinv_ref[...]
        gate = _row_scale(jnp.dot(xb, wg_ref[...], preferred_element_type=F32), rinv)
        up = _row_scale(jnp.dot(xb, wu_ref[...], preferred_element_type=F32), rinv)
        act = (gate * jax.nn.sigmoid(gate) * up).astype(BF16)
        update = jnp.dot(act, wo_ref[...], preferred_element_type=F32)
        o_ref[...] = (x_ref[...] if first else o_ref[...]) + update

        if cast_next:
            wi_next_bf16_ref[...] = (wi_next_gain_ref[...] * wi_next_ref[...]).astype(BF16)
            wo_next_bf16_ref[...] = (0.5 * wo_next_ref[...]).astype(BF16)
        for (src_ref, gain_ref), dst_ref in zip(extra_refs, extra_bf16_refs):
            w = src_ref[...]
            dst_ref[...] = (w if gain_ref is None else gain_ref[...] * w).astype(BF16)

    pl.when(j == 0)(functools.partial(step, True))
    pl.when(j > 0)(functools.partial(step, False))

    if final_norm:
        @pl.when(j == n_ff_tiles - 1)
        def _():
            o_ref[...] = _rms(o_ref[...], gf_ref[...])


BF16_SUBLANES = 16


def _ffn(h, wi, wo_half, final_gain=None, next_weights=None, cast_extra=(), *, tm=1024, tf=512):
    wo = wo_half
    t, d = h.shape
    tm = min(tm, t)
    d_ff = wo.shape[0]
    n_ff_tiles = d_ff // tf
    n_row_tiles = t // tm
    assert t % tm == 0 and d_ff % tf == 0
    in_specs = [
        pl.BlockSpec((tm, d), lambda i, j: (i, 0)),
        pl.BlockSpec((d, tf), lambda i, j: (0, j)),
        pl.BlockSpec((d, tf), lambda i, j: (0, j + n_ff_tiles)),
        pl.BlockSpec((tf, d), lambda i, j: (j, 0)),
    ]
    args = [h, wi, wi, wo]
    out_specs = [pl.BlockSpec((tm, d), lambda i, j: (i, 0))]
    out_shape = [jax.ShapeDtypeStruct((t, d), F32)]
    pipelined = [((tm, d), F32), ((tm, d), F32), ((d, tf), BF16), ((d, tf), BF16), ((tf, d), BF16)]
    if final_gain is not None:
        in_specs.append(pl.BlockSpec((1, d), lambda i, j: (0, 0)))
        args.append(final_gain.reshape(1, d))
    if next_weights is not None:
        gain_next, wi_next, wo_next = next_weights
        assert wi_next.shape == wi.shape and wo_next.shape == wo.shape
        assert d % n_row_tiles == 0 and wi.shape[1] % n_ff_tiles == 0
        wi_block = (d // n_row_tiles, wi.shape[1] // n_ff_tiles)
        wo_block = (d_ff // n_ff_tiles, d // n_row_tiles)
        wi_spec = pl.BlockSpec(wi_block, lambda i, j: (i, j))
        wo_spec = pl.BlockSpec(wo_block, lambda i, j: (j, i))
        in_specs += [wi_spec, pl.BlockSpec((wi_block[0], 1), lambda i, j: (i, 0)), wo_spec]
        args += [wi_next, gain_next.reshape(d, 1), wo_next]
        out_specs += [wi_spec, wo_spec]
        out_shape += [jax.ShapeDtypeStruct(wi.shape, BF16), jax.ShapeDtypeStruct(wo.shape, BF16)]
        pipelined += [(wi_block, F32), (wo_block, F32), (wi_block, BF16), (wo_block, BF16), ((wi_block[0], LANES), F32)]
    extra_out_specs, extra_out_shape = [], []
    for w, row_gain in cast_extra:
        rows, cols = w.shape
        n_sub = max(s for s in range(1, n_ff_tiles + 1)
                    if rows % (n_row_tiles * s * BF16_SUBLANES) == 0)
        block_rows = rows // (n_row_tiles * n_sub)

        def row_block(i, j, n_sub=n_sub):
            return (i * n_sub + jnp.minimum(j, n_sub - 1), 0)

        spec = pl.BlockSpec((block_rows, cols), row_block)
        in_specs.append(spec)
        args.append(w)
        if row_gain is not None:
            in_specs.append(pl.BlockSpec((block_rows, 1), row_block))
            args.append(row_gain.reshape(rows, 1))
            pipelined.append(((block_rows, LANES), F32))
        extra_out_specs.append(spec)
        extra_out_shape.append(jax.ShapeDtypeStruct(w.shape, BF16))
        pipelined += [((block_rows, cols), F32), ((block_rows, cols), BF16)]
    out_specs += extra_out_specs
    out_shape += extra_out_shape
    outs = pl.pallas_call(
        functools.partial(_ffn_kernel, n_ff_tiles=n_ff_tiles, final_norm=final_gain is not None,
                          cast_next=next_weights is not None,
                          cast_extra_gains=tuple(g is not None for _, g in cast_extra)),
        grid=(n_row_tiles, n_ff_tiles),
        in_specs=in_specs,
        out_specs=out_specs,
        out_shape=out_shape,
        scratch_shapes=[pltpu.VMEM((tm, d), BF16), pltpu.VMEM((tm, LANES), F32)],
        compiler_params=_compiler_params(
            ("parallel", "arbitrary"), pipelined=pipelined,
            resident=[((tm, d), BF16), ((tm, LANES), F32), ((tm, tf), F32), ((tm, tf), F32), ((tm, tf), F32)]),
        name="ffn",
    )(*args)
    n_next = 0 if next_weights is None else 2
    return outs[0], (tuple(outs[1:3]) if n_next else None), tuple(outs[1 + n_next:])


def _norm_matmul_kernel(*refs, gelu, side, scaled_tiles, scale):
    if side:
        x_ref, w_ref, w2_ref, o_ref, o2_ref, xb_ref, rinv_ref = refs
    else:
        x_ref, w_ref, o_ref, xb_ref, rinv_ref = refs
    j = pl.program_id(1)

    def step(first):
        if first:
            _rms_prepare(x_ref[...], rinv_ref, xb_ref)
            if side:
                o2_ref[...] = _row_scale(jnp.dot(xb_ref[...], w2_ref[...], preferred_element_type=F32),
                                         rinv_ref[...])
        y = _row_scale(jnp.dot(xb_ref[...], w_ref[...], preferred_element_type=F32), rinv_ref[...])
        if gelu:
            y = 0.5 * y * (1.0 + lax.erf(y * np.float32(math.sqrt(0.5))))
        if scaled_tiles:
            y = y * jnp.where(j < scaled_tiles, np.float32(scale), np.float32(1.0))
        o_ref[...] = y.astype(o_ref.dtype)

    pl.when(j == 0)(functools.partial(step, True))
    pl.when(j > 0)(functools.partial(step, False))


def _norm_matmul(h, w, w_side=None, *, n_cols=None, gelu=False, scaled_cols=0, scale=1.0, tm=1024, tn=1024):
    t, d = h.shape
    tm = min(tm, t)
    n = w.shape[1] if n_cols is None else n_cols
    assert t % tm == 0 and n % tn == 0 and n <= w.shape[1] and scaled_cols % tn == 0
    side = w_side is not None
    in_specs = [
        pl.BlockSpec((tm, d), lambda i, j: (i, 0)),
        pl.BlockSpec((d, tn), lambda i, j: (0, j)),
    ]
    args = [h, w]
    out_specs = pl.BlockSpec((tm, tn), lambda i, j: (i, j))
    out_shape = jax.ShapeDtypeStruct((t, n), BF16)
    pipelined = [((tm, d), F32), ((d, tn), BF16), ((tm, tn), BF16)]
    if side:
        n2 = w_side.shape[1]
        assert n2 == LANES
        in_specs.append(pl.BlockSpec((d, n2), lambda i, j: (0, 0)))
        args.append(w_side)
        out_specs = [out_specs, pl.BlockSpec((tm, n2), lambda i, j: (i, 0))]
        out_shape = [out_shape, jax.ShapeDtypeStruct((t, n2), F32)]
        pipelined += [((d, n2), BF16), ((tm, n2), F32)]
    return pl.pallas_call(
        functools.partial(_norm_matmul_kernel, gelu=gelu, side=side, scaled_tiles=scaled_cols // tn, scale=scale),
        grid=(t // tm, n // tn),
        in_specs=in_specs,
        out_specs=out_specs,
        out_shape=out_shape,
        scratch_shapes=[pltpu.VMEM((tm, d), BF16), pltpu.VMEM((tm, LANES), F32)],
        compiler_params=_compiler_params(
            ("parallel", "arbitrary"), pipelined=pipelined,
            resident=[((tm, d), BF16), ((tm, LANES), F32), ((tm, tn), F32), ((tm, tn), F32)]),
        name="norm_matmul",
    )(*args)


def _matmul_residual_kernel(y_ref, w_ref, h_ref, o_ref):
    o_ref[...] = h_ref[...] + jnp.dot(y_ref[...], w_ref[...], preferred_element_type=F32)


def _matmul_residual(y, w, h, *, tm=512):
    t, k = y.shape
    d = w.shape[1]
    assert t % tm == 0
    return pl.pallas_call(
        _matmul_residual_kernel,
        grid=(t // tm,),
        in_specs=[
            pl.BlockSpec((tm, k), lambda i: (i, 0)),
            pl.BlockSpec((k, d), lambda i: (0, 0)),
            pl.BlockSpec((tm, d), lambda i: (i, 0)),
        ],
        out_specs=pl.BlockSpec((tm, d), lambda i: (i, 0)),
        out_shape=jax.ShapeDtypeStruct((t, d), F32),
        compiler_params=_compiler_params(
            ("parallel",),
            pipelined=[((tm, k), BF16), ((k, d), BF16), ((tm, d), F32), ((tm, d), F32)],
            resident=[((tm, d), F32)]),
        name="matmul_residual",
    )(y, w, h)


def _gmlp_mix_kernel(u_ref, v_ref, vg_ref, ws_ref, bias_ref, wout_ref, h_ref, o_ref, wsm_ref, y_ref,
                     *, n_chunks):
    @pl.when(pl.program_id(0) == 0)
    def _():
        row = lax.broadcasted_iota(jnp.int32, (CHUNK, CHUNK), 0)
        col = lax.broadcasted_iota(jnp.int32, (CHUNK, CHUNK), 1)
        for g in range(GMLP_GROUPS):
            wsm_ref[g] = jnp.where(col <= row, ws_ref[g], 0.0).astype(BF16)

    chunks_per_part = max(n_chunks // 2, 1)
    for c in range(n_chunks):
        rows = slice(c * CHUNK, (c + 1) * CHUNK)
        vn = _rms(v_ref[rows, :].astype(F32), vg_ref[...]).astype(BF16)
        for g in range(GMLP_GROUPS):
            cols = slice(g * LANES, (g + 1) * LANES)
            mixed = jnp.dot(wsm_ref[g], vn[:, cols], preferred_element_type=F32) + bias_ref[:, cols]
            y_ref[rows, cols] = (u_ref[rows, cols].astype(F32) * mixed).astype(BF16)
        if (c + 1) % chunks_per_part == 0:
            part = slice((c + 1 - chunks_per_part) * CHUNK, (c + 1) * CHUNK)
            o_ref[part, :] = h_ref[part, :] + jnp.dot(y_ref[part, :], wout_ref[...], preferred_element_type=F32)


def _gmlp_mix(z, v_gain, ws, bs, w_out, h, *, tm=512):
    t, d = h.shape
    tm = min(tm, t)
    width = z.shape[1] // 2
    assert width == GMLP_GROUPS * LANES and t % tm == 0 and tm % CHUNK == 0
    n_width_blocks = 1
    bias = jnp.repeat(bs.T, LANES, axis=1)
    return pl.pallas_call(
        functools.partial(_gmlp_mix_kernel, n_chunks=tm // CHUNK),
        grid=(t // tm,),
        in_specs=[
            pl.BlockSpec((tm, width), lambda i: (i, 0)),
            pl.BlockSpec((tm, width), lambda i: (i, n_width_blocks)),
            pl.BlockSpec((1, width), lambda i: (0, 0)),
            pl.BlockSpec((GMLP_GROUPS, CHUNK, CHUNK), lambda i: (0, 0, 0)),
            pl.BlockSpec((CHUNK, width), lambda i: (0, 0)),
            pl.BlockSpec((width, d), lambda i: (0, 0)),
            pl.BlockSpec((tm, d), lambda i: (i, 0)),
        ],
        out_specs=pl.BlockSpec((tm, d), lambda i: (i, 0)),
        out_shape=jax.ShapeDtypeStruct((t, d), F32),
        scratch_shapes=[pltpu.VMEM((GMLP_GROUPS, CHUNK, CHUNK), BF16), pltpu.VMEM((tm, width), BF16)],
        compiler_params=_compiler_params(
            ("arbitrary",),
            pipelined=[((tm, width), BF16), ((tm, width), BF16), ((1, width), F32),
                       ((GMLP_GROUPS, CHUNK, CHUNK), F32), ((CHUNK, width), F32), ((width, d), BF16),
                       ((tm, d), F32), ((tm, d), F32)],
            resident=[((GMLP_GROUPS, CHUNK, CHUNK), BF16), ((tm, width), BF16), ((tm, d), F32)]),
        name="gmlp_mix",
    )(z, z, v_gain.reshape(1, width), ws, bias, w_out, h)


def _rope_tables(seq):
    half = ROPE_DIM // 2
    inv_freq = ROPE_THETA ** (-(jnp.arange(half, dtype=F32) * 2.0 / ROPE_DIM))
    ang = jnp.arange(seq, dtype=F32)[:, None] * inv_freq[None, :]
    cos, sin = jnp.cos(ang), jnp.sin(ang)
    rest = jnp.zeros((seq, SWA_HEAD_DIM - ROPE_DIM), F32)
    cos_h = jnp.concatenate([cos, cos, rest + 1.0], axis=1)
    sin_h = jnp.concatenate([sin, sin, rest], axis=1)
    rot = np.zeros((LANES, LANES), np.float32)
    for i in range(LANES):
        if i % SWA_HEAD_DIM < half:
            rot[i + half, i] = -1.0
        elif i % SWA_HEAD_DIM < ROPE_DIM:
            rot[i - half, i] = 1.0
    return (jnp.concatenate([cos_h, cos_h], axis=1), jnp.concatenate([sin_h, sin_h], axis=1),
            jnp.asarray(rot, BF16))


def _swa_kernel(sinks_ref, q_ref, kc_ref, kp_ref, vc_ref, vp_ref, cos_ref, sin_ref, rot_ref, bias_ref, o_ref):
    n = pl.program_id(1)
    log2e = np.float32(math.log2(math.e))
    cur = pl.ds(pl.multiple_of(n * CHUNK, CHUNK), CHUNK)
    prev = pl.ds(pl.multiple_of(jnp.maximum(n - 1, 0) * CHUNK, CHUNK), CHUNK)
    low_lanes = lax.broadcasted_iota(jnp.int32, (CHUNK, LANES), 1) < SWA_HEAD_DIM
    pairs = SWA_Q_PER_KV // 2
    q_scale = np.float32(SWA_HEAD_DIM ** -0.5 * math.log2(math.e))

    n_q_blocks = SWA_KV_HEADS * pairs
    pieces = [q_ref[0, :, b * LANES:(b + 1) * LANES] for b in range(n_q_blocks)]
    pieces += [kc_ref[0, :, g * LANES:(g + 1) * LANES] for g in range(SWA_KV_HEADS)]
    pieces += [kp_ref[0, :, g * LANES:(g + 1) * LANES] for g in range(SWA_KV_HEADS)]
    partner = jnp.dot(jnp.concatenate(pieces, axis=0), rot_ref[...], preferred_element_type=F32)

    def rope(idx, rows):
        return (pieces[idx].astype(F32) * cos_ref[rows, :]
                + partner[idx * CHUNK:(idx + 1) * CHUNK] * sin_ref[rows, :])

    scores = []
    for g in range(SWA_KV_HEADS):
        k = jnp.concatenate([rope(n_q_blocks + SWA_KV_HEADS + g, prev), rope(n_q_blocks + g, cur)],
                            axis=0).astype(BF16)
        stacked = []
        for pair in range(pairs):
            q = rope(g * pairs + pair, cur) * q_scale
            stacked += [jnp.where(low_lanes, q, 0.0), jnp.where(low_lanes, 0.0, q)]
        q8 = jnp.concatenate(stacked, axis=0).astype(BF16)
        scores.append(lax.dot_general(q8, k, (((1,), (1,)), ((), ())), preferred_element_type=F32))

    for g in range(SWA_KV_HEADS):
        lanes_g = slice(g * LANES, (g + 1) * LANES)
        v = jnp.concatenate([vp_ref[0, :, lanes_g], vc_ref[0, :, lanes_g]], axis=0)
        s = scores[g]
        probs, inv_denoms = [], []
        for h in range(SWA_Q_PER_KV):
            rows = slice(h * CHUNK, (h + 1) * CHUNK)
            s_h = s[rows] + bias_ref[0]
            sink = sinks_ref[g * SWA_Q_PER_KV + h] * log2e
            m = jnp.maximum(jnp.max(s_h, axis=1, keepdims=True), sink)
            p = jnp.exp2(s_h - m)
            inv_denoms.append(1.0 / (jnp.sum(p, axis=1, keepdims=True) + jnp.exp2(sink - m)))
            probs.append(p.astype(BF16))
        o8 = jnp.dot(jnp.concatenate(probs, axis=0), v, preferred_element_type=F32)
        for pair in range(pairs):
            lanes_q = slice((g * pairs + pair) * LANES, (g * pairs + pair + 1) * LANES)
            lo = o8[2 * pair * CHUNK:(2 * pair + 1) * CHUNK] * inv_denoms[2 * pair]
            hi = o8[(2 * pair + 1) * CHUNK:(2 * pair + 2) * CHUNK] * inv_denoms[2 * pair + 1]
            o_ref[0, :, lanes_q] = jnp.where(low_lanes, lo, hi).astype(o_ref.dtype)


def _swa_attention(qkv, sinks, batch, seq):
    nb = seq // CHUNK
    qw = SWA_Q_HEADS * SWA_HEAD_DIM
    kw = SWA_KV_HEADS * LANES
    qkv = qkv.reshape(batch, seq, qw + 2 * kw)
    cos, sin, rot = _rope_tables(seq)
    k_blk = qw // kw
    table_spec = pl.BlockSpec((seq, LANES), lambda b, n: (0, 0))
    r = np.arange(CHUNK)[:, None]
    c = np.arange(2 * CHUNK)[None, :]
    band = (c > r) & (c <= r + CHUNK)
    bias = jnp.asarray(np.where(np.stack([band & (c >= CHUNK), band]), 0.0, NEG_INF), F32)
    return pl.pallas_call(
        _swa_kernel,
        grid=(batch, nb),
        in_specs=[
            pl.BlockSpec(memory_space=pltpu.SMEM),
            pl.BlockSpec((1, CHUNK, qw), lambda b, n: (b, n, 0)),
            pl.BlockSpec((1, CHUNK, kw), lambda b, n: (b, n, k_blk)),
            pl.BlockSpec((1, CHUNK, kw), lambda b, n: (b, jnp.maximum(n - 1, 0), k_blk)),
            pl.BlockSpec((1, CHUNK, kw), lambda b, n: (b, n, k_blk + 1)),
            pl.BlockSpec((1, CHUNK, kw), lambda b, n: (b, jnp.maximum(n - 1, 0), k_blk + 1)),
            table_spec, table_spec, pl.BlockSpec((LANES, LANES), lambda b, n: (0, 0)),
            pl.BlockSpec((1, CHUNK, 2 * CHUNK), lambda b, n: (jnp.minimum(n, 1), 0, 0)),
        ],
        out_specs=pl.BlockSpec((1, CHUNK, qw), lambda b, n: (b, n, 0)),
        out_shape=jax.ShapeDtypeStruct((batch, seq, qw), BF16),
        compiler_params=_compiler_params(
            ("parallel", "arbitrary"),
            pipelined=[((CHUNK, qw), BF16)] * 2 + [((CHUNK, kw), BF16)] * 4 + [((seq, LANES), F32)] * 2
                      + [((CHUNK, 2 * CHUNK), F32)],
            resident=[((SWA_Q_PER_KV * CHUNK, 2 * CHUNK), F32)] * 6),
        name="swa_attention",
    )(sinks, qkv, qkv, qkv, qkv, qkv, cos, sin, rot, bias).reshape(batch * seq, qw)


FOX_CUMSUM_CHUNK = 256
FOX_Q_SCALE = FOX_HEAD_DIM ** -0.5 * math.log2(math.e)


def _fox_decay_kernel(fl_ref, bf_ref, o_ref, *, n_chunks):
    cc = FOX_CUMSUM_CHUNK
    row = lax.broadcasted_iota(jnp.int32, (cc, cc), 0)
    col = lax.broadcasted_iota(jnp.int32, (cc, cc), 1)
    lower = jnp.where(col <= row, 1.0, 0.0).astype(BF16)
    carry = jnp.zeros((1, LANES), F32)
    for c in range(n_chunks):
        rows = slice(c * cc, (c + 1) * cc)
        x = jax.nn.log_sigmoid(fl_ref[0, rows, :] + bf_ref[...])
        hi = x.astype(BF16)
        rest = x - hi.astype(F32)
        mid = rest.astype(BF16)
        lo = (rest - mid.astype(F32)).astype(BF16)
        cs = (jnp.dot(lower, hi, preferred_element_type=F32) + jnp.dot(lower, mid, preferred_element_type=F32)
              + jnp.dot(lower, lo, preferred_element_type=F32)) + carry
        o_ref[0, rows, :] = cs
        carry = cs[cc - 1:cc, :]


def _fox_decay(fl, b_f, batch, seq):
    assert seq % FOX_CUMSUM_CHUNK == 0
    bf = jnp.zeros((1, LANES), F32).at[0, :FOX_HEADS].set(b_f)
    return pl.pallas_call(
        functools.partial(_fox_decay_kernel, n_chunks=seq // FOX_CUMSUM_CHUNK),
        grid=(batch,),
        in_specs=[pl.BlockSpec((1, seq, LANES), lambda b: (b, 0, 0)), pl.BlockSpec((1, LANES), lambda b: (0, 0))],
        out_specs=pl.BlockSpec((1, seq, LANES), lambda b: (b, 0, 0)),
        out_shape=jax.ShapeDtypeStruct((batch, seq, LANES), F32),
        compiler_params=_compiler_params(
            ("parallel",), pipelined=[((seq, LANES), F32)] * 2, resident=[((seq, LANES), F32)]),
        name="fox_decay",
    )(fl.reshape(batch, seq, LANES), bf)


def _fox_kernel(q_ref, k_ref, v_ref, dq_ref, dk_ref, o_ref, *, tq, n_tiles):
    head = pl.program_id(1)
    log2e = np.float32(math.log2(math.e))
    head_lane = lax.broadcasted_iota(jnp.int32, (1, LANES), 1) == head
    row = lax.broadcasted_iota(jnp.int32, (tq, tq), 0)
    col = lax.broadcasted_iota(jnp.int32, (tq, tq), 1)
    causal = col <= row

    for i in range(n_tiles):
        rows = slice(i * tq, (i + 1) * tq)
        dq = jnp.sum(jnp.where(head_lane, dq_ref[0, rows, :], 0.0), axis=1, keepdims=True) * log2e
        q = q_ref[0, rows, :]
        m = l = acc = None
        for kt in range(i + 1):
            keys = slice(kt * tq, (kt + 1) * tq)
            s = lax.dot_general(q, k_ref[0, keys, :], (((1,), (1,)), ((), ())), preferred_element_type=F32)
            s = s + dq - dk_ref[0, 0, kt:kt + 1, :] * log2e
            if kt == i:
                s = jnp.where(causal, s, NEG_INF)
            m_tile = jnp.max(s, axis=1, keepdims=True)
            m_new = m_tile if m is None else jnp.maximum(m, m_tile)
            p = jnp.exp2(s - m_new)
            l_tile = jnp.sum(p, axis=1, keepdims=True)
            pv = jnp.dot(p.astype(BF16), v_ref[0, keys, :], preferred_element_type=F32)
            if m is None:
                l, acc = l_tile, pv
            else:
                alpha = jnp.exp2(m - m_new)
                l, acc = alpha * l + l_tile, alpha * acc + pv
            m = m_new
        o_ref[0, rows, :] = (acc / l).astype(o_ref.dtype)


def _fox_attention(qkv, dec, batch, seq, *, tq=512):
    width = FOX_HEADS * FOX_HEAD_DIM
    tq = min(tq, seq)
    assert seq % tq == 0
    qkv = qkv.reshape(batch, seq, 3 * width)
    dec_k = dec[:, :, :FOX_HEADS].transpose(0, 2, 1).reshape(batch, FOX_HEADS, seq // tq, tq)

    def head_spec(first_block):
        return pl.BlockSpec((1, seq, FOX_HEAD_DIM), lambda b, h: (b, 0, first_block + h))

    return pl.pallas_call(
        functools.partial(_fox_kernel, tq=tq, n_tiles=seq // tq),
        grid=(batch, FOX_HEADS),
        in_specs=[
            head_spec(0), head_spec(FOX_HEADS), head_spec(2 * FOX_HEADS),
            pl.BlockSpec((1, seq, LANES), lambda b, h: (b, 0, 0)),
            pl.BlockSpec((1, 1, seq // tq, tq), lambda b, h: (b, h, 0, 0)),
        ],
        out_specs=head_spec(0),
        out_shape=jax.ShapeDtypeStruct((batch, seq, width), BF16),
        compiler_params=_compiler_params(
            ("parallel", "parallel"),
            pipelined=[((seq, FOX_HEAD_DIM), BF16)] * 4 + [((seq, LANES), F32), ((seq // tq, tq), F32)],
            resident=[((tq, tq), F32)] * 12),
        name="fox_attention",
    )(qkv, qkv, qkv, dec, dec_k).reshape(batch * seq, width)


def _gmlp_layer(h, w_in, v_gain, ws, bs, w_out):
    z = _norm_matmul(h, w_in, gelu=True)
    return _gmlp_mix(z, v_gain, ws, bs, w_out, h)


def _swa_layer(h, w_in, sinks, w_out, batch, seq):
    d = h.shape[1]
    qw = SWA_Q_HEADS * SWA_HEAD_DIM
    kvw = SWA_KV_HEADS * SWA_HEAD_DIM

    def duplicate_heads(w):
        w = w.reshape(d, SWA_KV_HEADS, 1, SWA_HEAD_DIM)
        return jnp.broadcast_to(w, (d, SWA_KV_HEADS, 2, SWA_HEAD_DIM)).reshape(d, SWA_KV_HEADS * LANES)

    w = jnp.concatenate([w_in[:, :qw], duplicate_heads(w_in[:, qw:qw + kvw]),
                         duplicate_heads(w_in[:, qw + kvw:])], axis=1)
    qkv = _norm_matmul(h, w)
    o = _swa_attention(qkv, sinks, batch, seq)
    return _matmul_residual(o, w_out, h)


def _fox_layer(h, norm, w_in, b_f, w_out, batch, seq):
    d = h.shape[1]
    width = FOX_HEADS * FOX_HEAD_DIM
    w_in = (norm[:, None] * w_in).astype(BF16)
    w_f = jnp.zeros((d, LANES), BF16).at[:, :FOX_HEADS].set(w_in[:, 3 * width:])
    qkv, fl = _norm_matmul(h, w_in, w_f, n_cols=3 * width, scaled_cols=width, scale=FOX_Q_SCALE)
    dec = _fox_decay(fl, b_f, batch, seq)
    o = _fox_attention(qkv, dec, batch, seq)
    return _matmul_residual(o, w_out, h)


def kernel(x, l0_ffn1_norm, l0_ffn1_wi, l0_ffn1_wo, l0_mix_norm, l0_mix_win, l0_gmlp_vnorm, l0_gmlp_ws, l0_gmlp_bs, l0_mix_wout, l0_ffn2_norm, l0_ffn2_wi, l0_ffn2_wo, l1_ffn1_norm, l1_ffn1_wi, l1_ffn1_wo, l1_mix_norm, l1_mix_win, l1_swa_sinks, l1_mix_wout, l1_ffn2_norm, l1_ffn2_wi, l1_ffn2_wo, l2_ffn1_norm, l2_ffn1_wi, l2_ffn1_wo, l2_mix_norm, l2_mix_win, l2_fox_bf, l2_mix_wout, l2_ffn2_norm, l2_ffn2_wi, l2_ffn2_wo, l3_ffn1_norm, l3_ffn1_wi, l3_ffn1_wo, l3_mix_norm, l3_mix_win, l3_gmlp_vnorm, l3_gmlp_ws, l3_gmlp_bs, l3_mix_wout, l3_ffn2_norm, l3_ffn2_wi, l3_ffn2_wo, final_norm):
    batch, seq, d = x.shape
    h = x.reshape(batch * seq, d)

    ffn_weights = [(l0_ffn1_norm, l0_ffn1_wi, l0_ffn1_wo), (l0_ffn2_norm, l0_ffn2_wi, l0_ffn2_wo),
                   (l1_ffn1_norm, l1_ffn1_wi, l1_ffn1_wo), (l1_ffn2_norm, l1_ffn2_wi, l1_ffn2_wo),
                   (l2_ffn1_norm, l2_ffn1_wi, l2_ffn1_wo), (l2_ffn2_norm, l2_ffn2_wi, l2_ffn2_wo),
                   (l3_ffn1_norm, l3_ffn1_wi, l3_ffn1_wo), (l3_ffn2_norm, l3_ffn2_wi, l3_ffn2_wo)]
    gain0, wi0, wo0 = ffn_weights[0]
    cast = [((gain0[:, None] * wi0).astype(BF16), (0.5 * wo0).astype(BF16))]

    def ffn(h, final_gain=None, cast_extra=()):
        index = len(cast) - 1
        wi, wo_half = cast[index]
        next_weights = ffn_weights[index + 1] if index + 1 < len(ffn_weights) else None
        h, cast_next, extra = _ffn(h, wi, wo_half, final_gain, next_weights, cast_extra)
        cast.append(cast_next)
        return h, extra

    h, (w_in, w_out) = ffn(h, cast_extra=((l0_mix_win, l0_mix_norm), (l0_mix_wout, None)))
    h = _gmlp_layer(h, w_in, l0_gmlp_vnorm, l0_gmlp_ws, l0_gmlp_bs, w_out)
    h, _ = ffn(h)

    h, (w_in, w_out) = ffn(h, cast_extra=((l1_mix_win, l1_mix_norm), (l1_mix_wout, None)))
    h = _swa_layer(h, w_in, l1_swa_sinks, w_out, batch, seq)
    h, _ = ffn(h)

    h, (w_out,) = ffn(h, cast_extra=((l2_mix_wout, None),))
    h = _fox_layer(h, l2_mix_norm, l2_mix_win, l2_fox_bf, w_out, batch, seq)
    h, _ = ffn(h)

    h, (w_in, w_out) = ffn(h, cast_extra=((l3_mix_win, l3_mix_norm), (l3_mix_wout, None)))
    h = _gmlp_layer(h, w_in, l3_gmlp_vnorm, l3_gmlp_ws, l3_gmlp_bs, w_out)
    h, _ = ffn(h, final_norm)
    return h.reshape(batch, seq, d)
```

```python
import functools
import math

import jax
import jax.numpy as jnp
import numpy as np
from jax import lax
from jax.experimental import pallas as pl
from jax.experimental.pallas import tpu as pltpu

F32 = jnp.float32
BF16 = jnp.bfloat16

NORM_EPS = 1e-5
NEG_INF = -1e30
CHUNK = 128
GMLP_GROUPS = 16
SWA_HEAD_DIM = 64
SWA_Q_HEADS = 32
SWA_KV_HEADS = 4
SWA_Q_PER_KV = SWA_Q_HEADS // SWA_KV_HEADS
ROPE_THETA = 500000.0
ROPE_DIM = SWA_HEAD_DIM // 4
FOX_HEAD_DIM = 128
FOX_HEADS = 16

LANES = 128
V7X_VMEM_BYTES = 64 * 1024 * 1024
V7X_VMEM_USABLE_BYTES = V7X_VMEM_BYTES - 4 * 1024 * 1024
V7X_DEFAULT_SCOPED_VMEM_BYTES = 32 * 1024 * 1024
SPILL_ALLOWANCE_DIVISOR = 4


def _nbytes(shape, dtype):
    return math.prod(shape) * jnp.dtype(dtype).itemsize


def _compiler_params(semantics, pipelined, resident):
    need = 2 * sum(_nbytes(s, d) for s, d in pipelined) + sum(_nbytes(s, d) for s, d in resident)
    need += need // SPILL_ALLOWANCE_DIVISOR
    limit = min(V7X_VMEM_USABLE_BYTES, max(V7X_DEFAULT_SCOPED_VMEM_BYTES, need))
    return pltpu.CompilerParams(dimension_semantics=semantics, vmem_limit_bytes=limit)


def _rms(x, gain):
    return x * lax.rsqrt(jnp.mean(x * x, axis=-1, keepdims=True) + NORM_EPS) * gain


def _rms_prepare(x, rinv_ref, xb_ref):
    rinv = lax.rsqrt(jnp.mean(x * x, axis=-1, keepdims=True) + NORM_EPS)
    rinv_ref[...] = jnp.broadcast_to(rinv, rinv_ref.shape)
    xb_ref[...] = x.astype(BF16)


def _row_scale(y, rinv):
    return y * jnp.concatenate([rinv] * (y.shape[1] // LANES), axis=1)


def _ffn_kernel(*refs, n_ff_tiles, final_norm, cast_next, cast_extra_gains):
    refs = list(refs)
    x_ref, wg_ref, wu_ref, wo_ref = refs[:4]
    del refs[:4]
    gf_ref = refs.pop(0) if final_norm else None
    if cast_next:
        wi_next_ref, wi_next_gain_ref, wo_next_ref = refs[:3]
        del refs[:3]
    extra_refs = []
    for has_gain in cast_extra_gains:
        extra_refs.append((refs.pop(0), refs.pop(0) if has_gain else None))
    o_ref = refs.pop(0)
    if cast_next:
        wi_next_bf16_ref, wo_next_bf16_ref = refs[:2]
        del refs[:2]
    extra_bf16_refs = refs[:len(cast_extra_gains)]
    del refs[:len(cast_extra_gains)]
    xb_ref, rinv_ref = refs
    j = pl.program_id(1)

    def step(first):
        if first:
            _rms_prepare(x_ref[...], rinv_ref, xb_ref)
        xb = xb_ref[...]
        rinv = rinv_ref[...]
        gate = _row_scale(jnp.dot(xb, wg_ref[...], preferred_element_type=F32), rinv)
        up = _row_scale(jnp.dot(xb, wu_ref[...], preferred_element_type=F32), rinv)
        act = (gate * jax.nn.sigmoid(gate) * up).astype(BF16)
        update = jnp.dot(act, wo_ref[...], preferred_element_type=F32)
        o_ref[...] = (x_ref[...] if first else o_ref[...]) + update

        if cast_next:
            wi_next_bf16_ref[...] = (wi_next_gain_ref[...] * wi_next_ref[...]).astype(BF16)
            wo_next_bf16_ref[...] = (0.5 * wo_next_ref[...]).astype(BF16)
        for (src_ref, gain_ref), dst_ref in zip(extra_refs, extra_bf16_refs):
            w = src_ref[...]
            dst_ref[...] = (w if gain_ref is None else gain_ref[...] * w).astype(BF16)

    pl.when(j == 0)(functools.partial(step, True))
    pl.when(j > 0)(functools.partial(step, False))

    if final_norm:
        @pl.when(j == n_ff_tiles - 1)
        def _():
            o_ref[...] = _rms(o_ref[...], gf_ref[...])


BF16_SUBLANES = 16


def _ffn(h, wi, wo_half, final_gain=None, next_weights=None, cast_extra=(), *, tm=1024, tf=512):
    wo = wo_half
    t, d = h.shape
    tm = min(tm, t)
    d_ff = wo.shape[0]
    n_ff_tiles = d_ff // tf
    n_row_tiles = t // tm
    assert t % tm == 0 and d_ff % tf == 0
    in_specs = [
        pl.BlockSpec((tm, d), lambda i, j: (i, 0)),
        pl.BlockSpec((d, tf), lambda i, j: (0, j)),
        pl.BlockSpec((d, tf), lambda i, j: (0, j + n_ff_tiles)),
        pl.BlockSpec((tf, d), lambda i, j: (j, 0)),
    ]
    args = [h, wi, wi, wo]
    out_specs = [pl.BlockSpec((tm, d), lambda i, j: (i, 0))]
    out_shape = [jax.ShapeDtypeStruct((t, d), F32)]
    pipelined = [((tm, d), F32), ((tm, d), F32), ((d, tf), BF16), ((d, tf), BF16), ((tf, d), BF16)]
    if final_gain is not None:
        in_specs.append(pl.BlockSpec((1, d), lambda i, j: (0, 0)))
        args.append(final_gain.reshape(1, d))
    if next_weights is not None:
        gain_next, wi_next, wo_next = next_weights
        assert wi_next.shape == wi.shape and wo_next.shape == wo.shape
        assert d % n_row_tiles == 0 and wi.shape[1] % n_ff_tiles == 0
        wi_block = (d // n_row_tiles, wi.shape[1] // n_ff_tiles)
        wo_block = (d_ff // n_ff_tiles, d // n_row_tiles)
        wi_spec = pl.BlockSpec(wi_block, lambda i, j: (i, j))
        wo_spec = pl.BlockSpec(wo_block, lambda i, j: (j, i))
        in_specs += [wi_spec, pl.BlockSpec((wi_block[0], 1), lambda i, j: (i, 0)), wo_spec]
        args += [wi_next, gain_next.reshape(d, 1), wo_next]
        out_specs += [wi_spec, wo_spec]
        out_shape += [jax.ShapeDtypeStruct(wi.shape, BF16), jax.ShapeDtypeStruct(wo.shape, BF16)]
        pipelined += [(wi_block, F32), (wo_block, F32), (wi_block, BF16), (wo_block, BF16), ((wi_block[0], LANES), F32)]
    extra_out_specs, extra_out_shape = [], []
    for w, row_gain in cast_extra:
        rows, cols = w.shape
        n_sub = max(s for s in range(1, n_ff_tiles + 1)
                    if rows % (n_row_tiles * s * BF16_SUBLANES) == 0)
        block_rows = rows // (n_row_tiles * n_sub)

        def row_block(i, j, n_sub=n_sub):
            return (i * n_sub + jnp.minimum(j, n_sub - 1), 0)

        spec = pl.BlockSpec((block_rows, cols), row_block)
        in_specs.append(spec)
        args.append(w)
        if row_gain is not None:
            in_specs.append(pl.BlockSpec((block_rows, 1), row_block))
            args.append(row_gain.reshape(rows, 1))
            pipelined.append(((block_rows, LANES), F32))
        extra_out_specs.append(spec)
        extra_out_shape.append(jax.ShapeDtypeStruct(w.shape, BF16))
        pipelined += [((block_rows, cols), F32), ((block_rows, cols), BF16)]
    out_specs += extra_out_specs
    out_shape += extra_out_shape
    outs = pl.pallas_call(
        functools.partial(_ffn_kernel, n_ff_tiles=n_ff_tiles, final_norm=final_gain is not None,
                          cast_next=next_weights is not None,
                          cast_extra_gains=tuple(g is not None for _, g in cast_extra)),
        grid=(n_row_tiles, n_ff_tiles),
        in_specs=in_specs,
        out_specs=out_specs,
        out_shape=out_shape,
        scratch_shapes=[pltpu.VMEM((tm, d), BF16), pltpu.VMEM((tm, LANES), F32)],
        compiler_params=_compiler_params(
            ("parallel", "arbitrary"), pipelined=pipelined,
            resident=[((tm, d), BF16), ((tm, LANES), F32), ((tm, tf), F32), ((tm, tf), F32), ((tm, tf), F32)]),
        name="ffn",
    )(*args)
    n_next = 0 if next_weights is None else 2
    return outs[0], (tuple(outs[1:3]) if n_next else None), tuple(outs[1 + n_next:])


def _norm_matmul_kernel(*refs, gelu, side, scaled_tiles, scale):
    if side:
        x_ref, w_ref, w2_ref, o_ref, o2_ref, xb_ref, rinv_ref = refs
    else:
        x_ref, w_ref, o_ref, xb_ref, rinv_ref = refs
    j = pl.program_id(1)

    def step(first):
        if first:
            _rms_prepare(x_ref[...], rinv_ref, xb_ref)
            if side:
                o2_ref[...] = _row_scale(jnp.dot(xb_ref[...], w2_ref[...], preferred_element_type=F32),
                                         rinv_ref[...])
        y = _row_scale(jnp.dot(xb_ref[...], w_ref[...], preferred_element_type=F32), rinv_ref[...])
        if gelu:
            y = 0.5 * y * (1.0 + lax.erf(y * np.float32(math.sqrt(0.5))))
        if scaled_tiles:
            y = y * jnp.where(j < scaled_tiles, np.float32(scale), np.float32(1.0))
        o_ref[...] = y.astype(o_ref.dtype)

    pl.when(j == 0)(functools.partial(step, True))
    pl.when(j > 0)(functools.partial(step, False))


def _norm_matmul(h, w, w_side=None, *, n_cols=None, gelu=False, scaled_cols=0, scale=1.0, tm=1024, tn=1024):
    t, d = h.shape
    tm = min(tm, t)
    n = w.shape[1] if n_cols is None else n_cols
    assert t % tm == 0 and n % tn == 0 and n <= w.shape[1] and scaled_cols % tn == 0
    side = w_side is not None
    in_specs = [
        pl.BlockSpec((tm, d), lambda i, j: (i, 0)),
        pl.BlockSpec((d, tn), lambda i, j: (0, j)),
    ]
    args = [h, w]
    out_specs = pl.BlockSpec((tm, tn), lambda i, j: (i, j))
    out_shape = jax.ShapeDtypeStruct((t, n), BF16)
    pipelined = [((tm, d), F32), ((d, tn), BF16), ((tm, tn), BF16)]
    if side:
        n2 = w_side.shape[1]
        assert n2 == LANES
        in_specs.append(pl.BlockSpec((d, n2), lambda i, j: (0, 0)))
        args.append(w_side)
        out_specs = [out_specs, pl.BlockSpec((tm, n2), lambda i, j: (i, 0))]
        out_shape = [out_shape, jax.ShapeDtypeStruct((t, n2), F32)]
        pipelined += [((d, n2), BF16), ((tm, n2), F32)]
    return pl.pallas_call(
        functools.partial(_norm_matmul_kernel, gelu=gelu, side=side, scaled_tiles=scaled_cols // tn, scale=scale),
        grid=(t // tm, n // tn),
        in_specs=in_specs,
        out_specs=out_specs,
        out_shape=out_shape,
        scratch_shapes=[pltpu.VMEM((tm, d), BF16), pltpu.VMEM((tm, LANES), F32)],
        compiler_params=_compiler_params(
            ("parallel", "arbitrary"), pipelined=pipelined,
            resident=[((tm, d), BF16), ((tm, LANES), F32), ((tm, tn), F32), ((tm, tn), F32)]),
        name="norm_matmul",
    )(*args)


def _matmul_residual_kernel(y_ref, w_ref, h_ref, o_ref):
    o_ref[...] = h_ref[...] + jnp.dot(y_ref[...], w_ref[...], preferred_element_type=F32)


def _matmul_residual(y, w, h, *, tm=512):
    t, k = y.shape
    d = w.shape[1]
    assert t % tm == 0
    return pl.pallas_call(
        _matmul_residual_kernel,
        grid=(t // tm,),
        in_specs=[
            pl.BlockSpec((tm, k), lambda i: (i, 0)),
            pl.BlockSpec((k, d), lambda i: (0, 0)),
            pl.BlockSpec((tm, d), lambda i: (i, 0)),
        ],
        out_specs=pl.BlockSpec((tm, d), lambda i: (i, 0)),
        out_shape=jax.ShapeDtypeStruct((t, d), F32),
        compiler_params=_compiler_params(
            ("parallel",),
            pipelined=[((tm, k), BF16), ((k, d), BF16), ((tm, d), F32), ((tm, d), F32)],
            resident=[((tm, d), F32)]),
        name="matmul_residual",
    )(y, w, h)


def _gmlp_mix_kernel(u_ref, v_ref, vg_ref, ws_ref, bias_ref, wout_ref, h_ref, o_ref, wsm_ref, y_ref,
                     *, n_chunks):
    @pl.when(pl.program_id(0) == 0)
    def _():
        row = lax.broadcasted_iota(jnp.int32, (CHUNK, CHUNK), 0)
        col = lax.broadcasted_iota(jnp.int32, (CHUNK, CHUNK), 1)
        for g in range(GMLP_GROUPS):
            wsm_ref[g] = jnp.where(col <= row, ws_ref[g], 0.0).astype(BF16)

    chunks_per_part = max(n_chunks // 2, 1)
    for c in range(n_chunks):
        rows = slice(c * CHUNK, (c + 1) * CHUNK)
        vn = _rms(v_ref[rows, :].astype(F32), vg_ref[...]).astype(BF16)
        for g in range(GMLP_GROUPS):
            cols = slice(g * LANES, (g + 1) * LANES)
            mixed = jnp.dot(wsm_ref[g], vn[:, cols], preferred_element_type=F32) + bias_ref[:, cols]
            y_ref[rows, cols] = (u_ref[rows, cols].astype(F32) * mixed).astype(BF16)
        if (c + 1) % chunks_per_part == 0:
            part = slice((c + 1 - chunks_per_part) * CHUNK, (c + 1) * CHUNK)
            o_ref[part, :] = h_ref[part, :] + jnp.dot(y_ref[part, :], wout_ref[...], preferred_element_type=F32)


def _gmlp_mix(z, v_gain, ws, bs, w_out, h, *, tm=512):
    t, d = h.shape
    tm = min(tm, t)
    width = z.shape[1] // 2
    assert width == GMLP_GROUPS * LANES and t % tm == 0 and tm % CHUNK == 0
    bias = jnp.repeat(bs.T, LANES, axis=1)
    return pl.pallas_call(
        functools.partial(_gmlp_mix_kernel, n_chunks=tm // CHUNK),
        grid=(t // tm,),
        in_specs=[
            pl.BlockSpec((tm, width), lambda i: (i, 0)),
            pl.BlockSpec((tm, width), lambda i: (i, 1)),
            pl.BlockSpec((1, width), lambda i: (0, 0)),
            pl.BlockSpec((GMLP_GROUPS, CHUNK, CHUNK), lambda i: (0, 0, 0)),
            pl.BlockSpec((CHUNK, width), lambda i: (0, 0)),
            pl.BlockSpec((width, d), lambda i: (0, 0)),
            pl.BlockSpec((tm, d), lambda i: (i, 0)),
        ],
        out_specs=pl.BlockSpec((tm, d), lambda i: (i, 0)),
        out_shape=jax.ShapeDtypeStruct((t, d), F32),
        scratch_shapes=[pltpu.VMEM((GMLP_GROUPS, CHUNK, CHUNK), BF16), pltpu.VMEM((tm, width), BF16)],
        compiler_params=_compiler_params(
            ("arbitrary",),
            pipelined=[((tm, width), BF16), ((tm, width), BF16), ((1, width), F32),
                       ((GMLP_GROUPS, CHUNK, CHUNK), F32), ((CHUNK, width), F32), ((width, d), BF16),
                       ((tm, d), F32), ((tm, d), F32)],
            resident=[((GMLP_GROUPS, CHUNK, CHUNK), BF16), ((tm, width), BF16), ((tm, d), F32)]),
        name="gmlp_mix",
    )(z, z, v_gain.reshape(1, width), ws, bias, w_out, h)


def _rope_tables(seq):
    half = ROPE_DIM // 2
    inv_freq = ROPE_THETA ** (-(jnp.arange(half, dtype=F32) * 2.0 / ROPE_DIM))
    ang = jnp.arange(seq, dtype=F32)[:, None] * inv_freq[None, :]
    cos, sin = jnp.cos(ang), jnp.sin(ang)
    rest = jnp.zeros((seq, SWA_HEAD_DIM - ROPE_DIM), F32)
    cos_h = jnp.concatenate([cos, cos, rest + 1.0], axis=1)
    sin_h = jnp.concatenate([sin, sin, rest], axis=1)
    rot = np.zeros((LANES, LANES), np.float32)
    for i in range(LANES):
        if i % SWA_HEAD_DIM < half:
            rot[i + half, i] = -1.0
        elif i % SWA_HEAD_DIM < ROPE_DIM:
            rot[i - half, i] = 1.0
    return (jnp.concatenate([cos_h, cos_h], axis=1), jnp.concatenate([sin_h, sin_h], axis=1),
            jnp.asarray(rot, BF16))


def _swa_kernel(sinks_ref, q_ref, kc_ref, kp_ref, vc_ref, vp_ref, cos_ref, sin_ref, rot_ref, o_ref):
    n = pl.program_id(1)
    has_prev = n > 0
    in_current = (lax.broadcasted_iota(jnp.int32, (CHUNK, CHUNK), 1)
                  <= lax.broadcasted_iota(jnp.int32, (CHUNK, CHUNK), 0))
    log2e = np.float32(math.log2(math.e))
    cur = pl.ds(pl.multiple_of(n * CHUNK, CHUNK), CHUNK)
    prev = pl.ds(pl.multiple_of(jnp.maximum(n - 1, 0) * CHUNK, CHUNK), CHUNK)
    low_lanes = lax.broadcasted_iota(jnp.int32, (CHUNK, LANES), 1) < SWA_HEAD_DIM
    pairs = SWA_Q_PER_KV // 2
    q_scale = np.float32(SWA_HEAD_DIM ** -0.5 * math.log2(math.e))

    n_q_blocks = SWA_KV_HEADS * pairs
    pieces = [q_ref[0, :, b * LANES:(b + 1) * LANES] for b in range(n_q_blocks)]
    pieces += [kc_ref[0, :, g * LANES:(g + 1) * LANES] for g in range(SWA_KV_HEADS)]
    pieces += [kp_ref[0, :, g * LANES:(g + 1) * LANES] for g in range(SWA_KV_HEADS)]
    partner = jnp.dot(jnp.concatenate(pieces, axis=0), rot_ref[...], preferred_element_type=F32)

    def rope(idx, rows):
        return (pieces[idx].astype(F32) * cos_ref[rows, :]
                + partner[idx * CHUNK:(idx + 1) * CHUNK] * sin_ref[rows, :])

    scores = []
    for g in range(SWA_KV_HEADS):
        k = jnp.concatenate([rope(n_q_blocks + SWA_KV_HEADS + g, prev), rope(n_q_blocks + g, cur)],
                            axis=0).astype(BF16)
        stacked = []
        for pair in range(pairs):
            q = rope(g * pairs + pair, cur) * q_scale
            stacked += [jnp.where(low_lanes, q, 0.0), jnp.where(low_lanes, 0.0, q)]
        q8 = jnp.concatenate(stacked, axis=0).astype(BF16)
        scores.append(lax.dot_general(q8, k, (((1,), (1,)), ((), ())), preferred_element_type=F32))

    for g in range(SWA_KV_HEADS):
        lanes_g = slice(g * LANES, (g + 1) * LANES)
        v = jnp.concatenate([vp_ref[0, :, lanes_g], vc_ref[0, :, lanes_g]], axis=0)
        s = scores[g]
        probs, inv_denoms = [], []
        for h in range(SWA_Q_PER_KV):
            rows = slice(h * CHUNK, (h + 1) * CHUNK)
            s_prev = jnp.where(has_prev, s[rows, :CHUNK], NEG_INF)
            folded = jnp.where(in_current, s[rows, CHUNK:], s_prev)
            sink = sinks_ref[g * SWA_Q_PER_KV + h] * log2e
            m = jnp.maximum(jnp.max(folded, axis=1, keepdims=True), sink)
            p = jnp.exp2(folded - m)
            inv_denoms.append(1.0 / (jnp.sum(p, axis=1, keepdims=True) + jnp.exp2(sink - m)))
            probs.append(jnp.concatenate([jnp.where(in_current, 0.0, p), jnp.where(in_current, p, 0.0)],
                                         axis=1).astype(BF16))
        o8 = jnp.dot(jnp.concatenate(probs, axis=0), v, preferred_element_type=F32)
        for pair in range(pairs):
            lanes_q = slice((g * pairs + pair) * LANES, (g * pairs + pair + 1) * LANES)
            lo = o8[2 * pair * CHUNK:(2 * pair + 1) * CHUNK] * inv_denoms[2 * pair]
            hi = o8[(2 * pair + 1) * CHUNK:(2 * pair + 2) * CHUNK] * inv_denoms[2 * pair + 1]
            o_ref[0, :, lanes_q] = jnp.where(low_lanes, lo, hi).astype(o_ref.dtype)


def _swa_attention(qkv, sinks, batch, seq):
    nb = seq // CHUNK
    qw = SWA_Q_HEADS * SWA_HEAD_DIM
    kw = SWA_KV_HEADS * LANES
    qkv = qkv.reshape(batch, seq, qw + 2 * kw)
    cos, sin, rot = _rope_tables(seq)
    k_blk = qw // kw
    table_spec = pl.BlockSpec((seq, LANES), lambda b, n: (0, 0))
    return pl.pallas_call(
        _swa_kernel,
        grid=(batch, nb),
        in_specs=[
            pl.BlockSpec(memory_space=pltpu.SMEM),
            pl.BlockSpec((1, CHUNK, qw), lambda b, n: (b, n, 0)),
            pl.BlockSpec((1, CHUNK, kw), lambda b, n: (b, n, k_blk)),
            pl.BlockSpec((1, CHUNK, kw), lambda b, n: (b, jnp.maximum(n - 1, 0), k_blk)),
            pl.BlockSpec((1, CHUNK, kw), lambda b, n: (b, n, k_blk + 1)),
            pl.BlockSpec((1, CHUNK, kw), lambda b, n: (b, jnp.maximum(n - 1, 0), k_blk + 1)),
            table_spec, table_spec, pl.BlockSpec((LANES, LANES), lambda b, n: (0, 0)),
        ],
        out_specs=pl.BlockSpec((1, CHUNK, qw), lambda b, n: (b, n, 0)),
        out_shape=jax.ShapeDtypeStruct((batch, seq, qw), BF16),
        compiler_params=_compiler_params(
            ("parallel", "arbitrary"),
            pipelined=[((CHUNK, qw), BF16)] * 2 + [((CHUNK, kw), BF16)] * 4 + [((seq, LANES), F32)] * 2,
            resident=[((SWA_Q_PER_KV * CHUNK, 2 * CHUNK), F32)] * 6),
        name="swa_attention",
    )(sinks, qkv, qkv, qkv, qkv, qkv, cos, sin, rot).reshape(batch * seq, qw)


FOX_CUMSUM_CHUNK = 256
FOX_Q_SCALE = FOX_HEAD_DIM ** -0.5 * math.log2(math.e)


def _fox_decay_kernel(fl_ref, bf_ref, o_ref, *, n_chunks):
    cc = FOX_CUMSUM_CHUNK
    row = lax.broadcasted_iota(jnp.int32, (cc, cc), 0)
    col = lax.broadcasted_iota(jnp.int32, (cc, cc), 1)
    lower = jnp.where(col <= row, 1.0, 0.0).astype(BF16)
    carry = jnp.zeros((1, LANES), F32)
    for c in range(n_chunks):
        rows = slice(c * cc, (c + 1) * cc)
        x = jax.nn.log_sigmoid(fl_ref[0, rows, :] + bf_ref[...])
        hi = x.astype(BF16)
        rest = x - hi.astype(F32)
        mid = rest.astype(BF16)
        lo = (rest - mid.astype(F32)).astype(BF16)
        cs = (jnp.dot(lower, hi, preferred_element_type=F32) + jnp.dot(lower, mid, preferred_element_type=F32)
              + jnp.dot(lower, lo, preferred_element_type=F32)) + carry
        o_ref[0, rows, :] = cs
        carry = cs[cc - 1:cc, :]


def _fox_decay(fl, b_f, batch, seq):
    assert seq % FOX_CUMSUM_CHUNK == 0
    bf = jnp.zeros((1, LANES), F32).at[0, :FOX_HEADS].set(b_f)
    return pl.pallas_call(
        functools.partial(_fox_decay_kernel, n_chunks=seq // FOX_CUMSUM_CHUNK),
        grid=(batch,),
        in_specs=[pl.BlockSpec((1, seq, LANES), lambda b: (b, 0, 0)), pl.BlockSpec((1, LANES), lambda b: (0, 0))],
        out_specs=pl.BlockSpec((1, seq, LANES), lambda b: (b, 0, 0)),
        out_shape=jax.ShapeDtypeStruct((batch, seq, LANES), F32),
        compiler_params=_compiler_params(
            ("parallel",), pipelined=[((seq, LANES), F32)] * 2, resident=[((seq, LANES), F32)]),
        name="fox_decay",
    )(fl.reshape(batch, seq, LANES), bf)


def _fox_kernel(q_ref, k_ref, v_ref, dq_ref, dk_ref, o_ref, *, tq, n_tiles, heads_per_step):
    log2e = np.float32(math.log2(math.e))
    lane = lax.broadcasted_iota(jnp.int32, (1, LANES), 1)
    row = lax.broadcasted_iota(jnp.int32, (tq, tq), 0)
    col = lax.broadcasted_iota(jnp.int32, (tq, tq), 1)
    causal = col <= row

    for local_head in range(heads_per_step):
        head_lane = lane == pl.program_id(1) * heads_per_step + local_head
        lanes = slice(local_head * FOX_HEAD_DIM, (local_head + 1) * FOX_HEAD_DIM)
        for i in range(n_tiles):
            rows = slice(i * tq, (i + 1) * tq)
            dq = jnp.sum(jnp.where(head_lane, dq_ref[0, rows, :], 0.0), axis=1, keepdims=True) * log2e
            q = q_ref[0, rows, lanes]
            m = l = acc = None
            for kt in range(i + 1):
                keys = slice(kt * tq, (kt + 1) * tq)
                s = lax.dot_general(q, k_ref[0, keys, lanes], (((1,), (1,)), ((), ())),
                                    preferred_element_type=F32)
                s = s + dq - dk_ref[0, local_head, kt:kt + 1, :] * log2e
                if kt == i:
                    s = jnp.where(causal, s, NEG_INF)
                m_tile = jnp.max(s, axis=1, keepdims=True)
                m_new = m_tile if m is None else jnp.maximum(m, m_tile)
                p = jnp.exp2(s - m_new)
                l_tile = jnp.sum(p, axis=1, keepdims=True)
                pv = jnp.dot(p.astype(BF16), v_ref[0, keys, lanes], preferred_element_type=F32)
                if m is None:
                    l, acc = l_tile, pv
                else:
                    alpha = jnp.exp2(m - m_new)
                    l, acc = alpha * l + l_tile, alpha * acc + pv
                m = m_new
            o_ref[0, rows, lanes] = (acc / l).astype(o_ref.dtype)


def _fox_attention(qkv, dec, batch, seq, *, tq=512, heads_per_step=2):
    width = FOX_HEADS * FOX_HEAD_DIM
    tq = min(tq, seq)
    assert seq % tq == 0 and FOX_HEADS % heads_per_step == 0
    head_groups = FOX_HEADS // heads_per_step
    group_width = heads_per_step * FOX_HEAD_DIM
    qkv = qkv.reshape(batch, seq, 3 * width)
    dec_k = dec[:, :, :FOX_HEADS].transpose(0, 2, 1).reshape(batch, FOX_HEADS, seq // tq, tq)

    def group_spec(first_block):
        return pl.BlockSpec((1, seq, group_width), lambda b, h: (b, 0, first_block + h))

    return pl.pallas_call(
        functools.partial(_fox_kernel, tq=tq, n_tiles=seq // tq, heads_per_step=heads_per_step),
        grid=(batch, head_groups),
        in_specs=[
            group_spec(0), group_spec(head_groups), group_spec(2 * head_groups),
            pl.BlockSpec((1, seq, LANES), lambda b, h: (b, 0, 0)),
            pl.BlockSpec((1, heads_per_step, seq // tq, tq), lambda b, h: (b, h, 0, 0)),
        ],
        out_specs=group_spec(0),
        out_shape=jax.ShapeDtypeStruct((batch, seq, width), BF16),
        compiler_params=_compiler_params(
            ("parallel", "parallel"),
            pipelined=[((seq, group_width), BF16)] * 4 + [((seq, LANES), F32), ((FOX_HEADS, tq), F32)],
            resident=[((tq, tq), F32)] * 12),
        name="fox_attention",
    )(qkv, qkv, qkv, dec, dec_k).reshape(batch * seq, width)


def _gmlp_layer(h, w_in, v_gain, ws, bs, w_out):
    z = _norm_matmul(h, w_in, gelu=True)
    return _gmlp_mix(z, v_gain, ws, bs, w_out, h)


def _swa_layer(h, w_in, sinks, w_out, batch, seq):
    d = h.shape[1]
    qw = SWA_Q_HEADS * SWA_HEAD_DIM
    kvw = SWA_KV_HEADS * SWA_HEAD_DIM

    def duplicate_heads(w):
        w = w.reshape(d, SWA_KV_HEADS, 1, SWA_HEAD_DIM)
        return jnp.broadcast_to(w, (d, SWA_KV_HEADS, 2, SWA_HEAD_DIM)).reshape(d, SWA_KV_HEADS * LANES)

    w = jnp.concatenate([w_in[:, :qw], duplicate_heads(w_in[:, qw:qw + kvw]),
                         duplicate_heads(w_in[:, qw + kvw:])], axis=1)
    qkv = _norm_matmul(h, w)
    o = _swa_attention(qkv, sinks, batch, seq)
    return _matmul_residual(o, w_out, h)


def _fox_layer(h, w_in, b_f, w_out, batch, seq):
    d = h.shape[1]
    width = FOX_HEADS * FOX_HEAD_DIM
    w_f = jnp.zeros((d, LANES), BF16).at[:, :FOX_HEADS].set(w_in[:, 3 * width:])
    qkv, fl = _norm_matmul(h, w_in, w_f, n_cols=3 * width, scaled_cols=width, scale=FOX_Q_SCALE)
    dec = _fox_decay(fl, b_f, batch, seq)
    o = _fox_attention(qkv, dec, batch, seq)
    return _matmul_residual(o, w_out, h)


def kernel(x, l0_ffn1_norm, l0_ffn1_wi, l0_ffn1_wo, l0_mix_norm, l0_mix_win, l0_gmlp_vnorm, l0_gmlp_ws, l0_gmlp_bs, l0_mix_wout, l0_ffn2_norm, l0_ffn2_wi, l0_ffn2_wo, l1_ffn1_norm, l1_ffn1_wi, l1_ffn1_wo, l1_mix_norm, l1_mix_win, l1_swa_sinks, l1_mix_wout, l1_ffn2_norm, l1_ffn2_wi, l1_ffn2_wo, l2_ffn1_norm, l2_ffn1_wi, l2_ffn1_wo, l2_mix_norm, l2_mix_win, l2_fox_bf, l2_mix_wout, l2_ffn2_norm, l2_ffn2_wi, l2_ffn2_wo, l3_ffn1_norm, l3_ffn1_wi, l3_ffn1_wo, l3_mix_norm, l3_mix_win, l3_gmlp_vnorm, l3_gmlp_ws, l3_gmlp_bs, l3_mix_wout, l3_ffn2_norm, l3_ffn2_wi, l3_ffn2_wo, final_norm):
    batch, seq, d = x.shape
    h = x.reshape(batch * seq, d)

    ffn_weights = [(l0_ffn1_norm, l0_ffn1_wi, l0_ffn1_wo), (l0_ffn2_norm, l0_ffn2_wi, l0_ffn2_wo),
                   (l1_ffn1_norm, l1_ffn1_wi, l1_ffn1_wo), (l1_ffn2_norm, l1_ffn2_wi, l1_ffn2_wo),
                   (l2_ffn1_norm, l2_ffn1_wi, l2_ffn1_wo), (l2_ffn2_norm, l2_ffn2_wi, l2_ffn2_wo),
                   (l3_ffn1_norm, l3_ffn1_wi, l3_ffn1_wo), (l3_ffn2_norm, l3_ffn2_wi, l3_ffn2_wo)]
    gain0, wi0, wo0 = ffn_weights[0]
    cast = [((gain0[:, None] * wi0).astype(BF16), (0.5 * wo0).astype(BF16))]

    def ffn(h, final_gain=None, cast_extra=()):
        index = len(cast) - 1
        wi, wo_half = cast[index]
        next_weights = ffn_weights[index + 1] if index + 1 < len(ffn_weights) else None
        h, cast_next, extra = _ffn(h, wi, wo_half, final_gain, next_weights, cast_extra)
        cast.append(cast_next)
        return h, extra

    h, (w_in, w_out) = ffn(h, cast_extra=((l0_mix_win, l0_mix_norm), (l0_mix_wout, None)))
    h = _gmlp_layer(h, w_in, l0_gmlp_vnorm, l0_gmlp_ws, l0_gmlp_bs, w_out)
    h, _ = ffn(h)

    h, (w_in, w_out) = ffn(h, cast_extra=((l1_mix_win, l1_mix_norm), (l1_mix_wout, None)))
    h = _swa_layer(h, w_in, l1_swa_sinks, w_out, batch, seq)
    h, _ = ffn(h)

    h, (w_in, w_out) = ffn(h, cast_extra=((l2_mix_win, l2_mix_norm), (l2_mix_wout, None)))
    h = _fox_layer(h, w_in, l2_fox_bf, w_out, batch, seq)
    h, _ = ffn(h)

    h, (w_in, w_out) = ffn(h, cast_extra=((l3_mix_win, l3_mix_norm), (l3_mix_wout, None)))
    h = _gmlp_layer(h, w_in, l3_gmlp_vnorm, l3_gmlp_ws, l3_gmlp_bs, w_out)
    h, _ = ffn(h, final_norm)
    return h.reshape(batch, seq, d)
```

```python
import functools
import math

import jax
import jax.numpy as jnp
import numpy as np
from jax import lax
from jax.experimental import pallas as pl
from jax.experimental.pallas import tpu as pltpu

F32 = jnp.float32
BF16 = jnp.bfloat16

NORM_EPS = 1e-5
NEG_INF = -1e30
CHUNK = 128
GMLP_GROUPS = 16
SWA_HEAD_DIM = 64
SWA_Q_HEADS = 32
SWA_KV_HEADS = 4
SWA_Q_PER_KV = SWA_Q_HEADS // SWA_KV_HEADS
ROPE_THETA = 500000.0
ROPE_DIM = SWA_HEAD_DIM // 4
FOX_HEAD_DIM = 128
FOX_HEADS = 16

LANES = 128
V7X_VMEM_BYTES = 64 * 1024 * 1024
V7X_VMEM_USABLE_BYTES = V7X_VMEM_BYTES - 4 * 1024 * 1024
V7X_DEFAULT_SCOPED_VMEM_BYTES = 32 * 1024 * 1024
SPILL_ALLOWANCE_DIVISOR = 4


def _nbytes(shape, dtype):
    return math.prod(shape) * jnp.dtype(dtype).itemsize


def _compiler_params(semantics, pipelined, resident):
    need = 2 * sum(_nbytes(s, d) for s, d in pipelined) + sum(_nbytes(s, d) for s, d in resident)
    need += need // SPILL_ALLOWANCE_DIVISOR
    limit = min(V7X_VMEM_USABLE_BYTES, max(V7X_DEFAULT_SCOPED_VMEM_BYTES, need))
    return pltpu.CompilerParams(dimension_semantics=semantics, vmem_limit_bytes=limit)


def _rms(x, gain):
    return x * lax.rsqrt(jnp.mean(x * x, axis=-1, keepdims=True) + NORM_EPS) * gain


def _rms_prepare(x, rinv_ref, xb_ref):
    rinv = lax.rsqrt(jnp.mean(x * x, axis=-1, keepdims=True) + NORM_EPS)
    rinv_ref[...] = jnp.broadcast_to(rinv, rinv_ref.shape)
    xb_ref[...] = x.astype(BF16)


def _row_scale(y, rinv):
    return y * jnp.concatenate([rinv] * (y.shape[1] // LANES), axis=1)


def _ffn_kernel(*refs, n_ff_tiles, final_norm, cast_next, cast_extra_gains):
    refs = list(refs)
    x_ref, wg_ref, wu_ref, wo_ref = refs[:4]
    del refs[:4]
    gf_ref = refs.pop(0) if final_norm else None
    if cast_next:
        wi_next_ref, wi_next_gain_ref, wo_next_ref = refs[:3]
        del refs[:3]
    extra_refs = []
    for has_gain in cast_extra_gains:
        extra_refs.append((refs.pop(0), refs.pop(0) if has_gain else None))
    o_ref = refs.pop(0)
    if cast_next:
        wi_next_bf16_ref, wo_next_bf16_ref = refs[:2]
        del refs[:2]
    extra_bf16_refs = refs[:len(cast_extra_gains)]
    del refs[:len(cast_extra_gains)]
    xb_ref, rinv_ref = refs
    j = pl.program_id(1)

    def step(first):
        if first:
            _rms_prepare(x_ref[...], rinv_ref, xb_ref)
        xb = xb_ref[...]
        rinv = rinv_ref[...]
        gate = _row_scale(jnp.dot(xb, wg_ref[...], preferred_element_type=F32), rinv)
        up = _row_scale(jnp.dot(xb, wu_ref[...], preferred_element_type=F32), rinv)
        act = (gate * jax.nn.sigmoid(gate) * up).astype(BF16)
        update = jnp.dot(act, wo_ref[...], preferred_element_type=F32)
        o_ref[...] = (x_ref[...] if first else o_ref[...]) + update

        if cast_next:
            wi_next_bf16_ref[...] = (wi_next_gain_ref[...] * wi_next_ref[...]).astype(BF16)
            wo_next_bf16_ref[...] = (0.5 * wo_next_ref[...]).astype(BF16)
        for (src_ref, gain_ref), dst_ref in zip(extra_refs, extra_bf16_refs):
            w = src_ref[...]
            dst_ref[...] = (w if gain_ref is None else gain_ref[...] * w).astype(BF16)

    pl.when(j == 0)(functools.partial(step, True))
    pl.when(j > 0)(functools.partial(step, False))

    if final_norm:
        @pl.when(j == n_ff_tiles - 1)
        def _():
            o_ref[...] = _rms(o_ref[...], gf_ref[...])


BF16_SUBLANES = 16


def _ffn(h, wi, wo_half, final_gain=None, next_weights=None, cast_extra=(), *, tm=1024, tf=512):
    wo = wo_half
    t, d = h.shape
    tm = min(tm, t)
    d_ff = wo.shape[0]
    n_ff_tiles = d_ff // tf
    n_row_tiles = t // tm
    assert t % tm == 0 and d_ff % tf == 0
    in_specs = [
        pl.BlockSpec((tm, d), lambda i, j: (i, 0)),
        pl.BlockSpec((d, tf), lambda i, j: (0, j)),
        pl.BlockSpec((d, tf), lambda i, j: (0, j + n_ff_tiles)),
        pl.BlockSpec((tf, d), lambda i, j: (j, 0)),
    ]
    args = [h, wi, wi, wo]
    out_specs = [pl.BlockSpec((tm, d), lambda i, j: (i, 0))]
    out_shape = [jax.ShapeDtypeStruct((t, d), F32)]
    pipelined = [((tm, d), F32), ((tm, d), F32), ((d, tf), BF16), ((d, tf), BF16), ((tf, d), BF16)]
    if final_gain is not None:
        in_specs.append(pl.BlockSpec((1, d), lambda i, j: (0, 0)))
        args.append(final_gain.reshape(1, d))
    if next_weights is not None:
        gain_next, wi_next, wo_next = next_weights
        assert wi_next.shape == wi.shape and wo_next.shape == wo.shape
        assert d % n_row_tiles == 0 and wi.shape[1] % n_ff_tiles == 0
        wi_block = (d // n_row_tiles, wi.shape[1] // n_ff_tiles)
        wo_block = (d_ff // n_ff_tiles, d // n_row_tiles)
        wi_spec = pl.BlockSpec(wi_block, lambda i, j: (i, j))
        wo_spec = pl.BlockSpec(wo_block, lambda i, j: (j, i))
        in_specs += [wi_spec, pl.BlockSpec((wi_block[0], 1), lambda i, j: (i, 0)), wo_spec]
        args += [wi_next, gain_next.reshape(d, 1), wo_next]
        out_specs += [wi_spec, wo_spec]
        out_shape += [jax.ShapeDtypeStruct(wi.shape, BF16), jax.ShapeDtypeStruct(wo.shape, BF16)]
        pipelined += [(wi_block, F32), (wo_block, F32), (wi_block, BF16), (wo_block, BF16), ((wi_block[0], LANES), F32)]
    extra_out_specs, extra_out_shape = [], []
    for w, row_gain in cast_extra:
        rows, cols = w.shape
        n_sub = max(s for s in range(1, n_ff_tiles + 1)
                    if rows % (n_row_tiles * s * BF16_SUBLANES) == 0)
        block_rows = rows // (n_row_tiles * n_sub)

        def row_block(i, j, n_sub=n_sub):
            return (i * n_sub + jnp.minimum(j, n_sub - 1), 0)

        spec = pl.BlockSpec((block_rows, cols), row_block)
        in_specs.append(spec)
        args.append(w)
        if row_gain is not None:
            in_specs.append(pl.BlockSpec((block_rows, 1), row_block))
            args.append(row_gain.reshape(rows, 1))
            pipelined.append(((block_rows, LANES), F32))
        extra_out_specs.append(spec)
        extra_out_shape.append(jax.ShapeDtypeStruct(w.shape, BF16))
        pipelined += [((block_rows, cols), F32), ((block_rows, cols), BF16)]
    out_specs += extra_out_specs
    out_shape += extra_out_shape
    outs = pl.pallas_call(
        functools.partial(_ffn_kernel, n_ff_tiles=n_ff_tiles, final_norm=final_gain is not None,
                          cast_next=next_weights is not None,
                          cast_extra_gains=tuple(g is not None for _, g in cast_extra)),
        grid=(n_row_tiles, n_ff_tiles),
        in_specs=in_specs,
        out_specs=out_specs,
        out_shape=out_shape,
        scratch_shapes=[pltpu.VMEM((tm, d), BF16), pltpu.VMEM((tm, LANES), F32)],
        compiler_params=_compiler_params(
            ("parallel", "arbitrary"), pipelined=pipelined,
            resident=[((tm, d), BF16), ((tm, LANES), F32), ((tm, tf), F32), ((tm, tf), F32), ((tm, tf), F32)]),
        name="ffn",
    )(*args)
    n_next = 0 if next_weights is None else 2
    return outs[0], (tuple(outs[1:3]) if n_next else None), tuple(outs[1 + n_next:])


def _norm_matmul_kernel(*refs, gelu, side, scaled_tiles, scale):
    if side:
        x_ref, w_ref, w2_ref, o_ref, o2_ref, xb_ref, rinv_ref = refs
    else:
        x_ref, w_ref, o_ref, xb_ref, rinv_ref = refs
    j = pl.program_id(1)

    def step(first):
        if first:
            _rms_prepare(x_ref[...], rinv_ref, xb_ref)
            if side:
                o2_ref[...] = _row_scale(jnp.dot(xb_ref[...], w2_ref[...], preferred_element_type=F32),
                                         rinv_ref[...])
        y = _row_scale(jnp.dot(xb_ref[...], w_ref[...], preferred_element_type=F32), rinv_ref[...])
        if gelu:
            y = 0.5 * y * (1.0 + lax.erf(y * np.float32(math.sqrt(0.5))))
        if scaled_tiles:
            y = y * jnp.where(j < scaled_tiles, np.float32(scale), np.float32(1.0))
        o_ref[...] = y.astype(o_ref.dtype)

    pl.when(j == 0)(functools.partial(step, True))
    pl.when(j > 0)(functools.partial(step, False))


def _norm_matmul(h, w, w_side=None, *, n_cols=None, gelu=False, scaled_cols=0, scale=1.0, tm=1024, tn=1024):
    t, d = h.shape
    tm = min(tm, t)
    n = w.shape[1] if n_cols is None else n_cols
    assert t % tm == 0 and n % tn == 0 and n <= w.shape[1] and scaled_cols % tn == 0
    side = w_side is not None
    in_specs = [
        pl.BlockSpec((tm, d), lambda i, j: (i, 0)),
        pl.BlockSpec((d, tn), lambda i, j: (0, j)),
    ]
    args = [h, w]
    out_specs = pl.BlockSpec((tm, tn), lambda i, j: (i, j))
    out_shape = jax.ShapeDtypeStruct((t, n), BF16)
    pipelined = [((tm, d), F32), ((d, tn), BF16), ((tm, tn), BF16)]
    if side:
        n2 = w_side.shape[1]
        assert n2 == LANES
        in_specs.append(pl.BlockSpec((d, n2), lambda i, j: (0, 0)))
        args.append(w_side)
        out_specs = [out_specs, pl.BlockSpec((tm, n2), lambda i, j: (i, 0))]
        out_shape = [out_shape, jax.ShapeDtypeStruct((t, n2), F32)]
        pipelined += [((d, n2), BF16), ((tm, n2), F32)]
    return pl.pallas_call(
        functools.partial(_norm_matmul_kernel, gelu=gelu, side=side, scaled_tiles=scaled_cols // tn, scale=scale),
        grid=(t // tm, n // tn),
        in_specs=in_specs,
        out_specs=out_specs,
        out_shape=out_shape,
        scratch_shapes=[pltpu.VMEM((tm, d), BF16), pltpu.VMEM((tm, LANES), F32)],
        compiler_params=_compiler_params(
            ("parallel", "arbitrary"), pipelined=pipelined,
            resident=[((tm, d), BF16), ((tm, LANES), F32), ((tm, tn), F32), ((tm, tn), F32)]),
        name="norm_matmul",
    )(*args)


def _matmul_residual_kernel(y_ref, w_ref, h_ref, o_ref):
    o_ref[...] = h_ref[...] + jnp.dot(y_ref[...], w_ref[...], preferred_element_type=F32)


def _matmul_residual(y, w, h, *, tm=512):
    t, k = y.shape
    d = w.shape[1]
    assert t % tm == 0
    return pl.pallas_call(
        _matmul_residual_kernel,
        grid=(t // tm,),
        in_specs=[
            pl.BlockSpec((tm, k), lambda i: (i, 0)),
            pl.BlockSpec((k, d), lambda i: (0, 0)),
            pl.BlockSpec((tm, d), lambda i: (i, 0)),
        ],
        out_specs=pl.BlockSpec((tm, d), lambda i: (i, 0)),
        out_shape=jax.ShapeDtypeStruct((t, d), F32),
        compiler_params=_compiler_params(
            ("parallel",),
            pipelined=[((tm, k), BF16), ((k, d), BF16), ((tm, d), F32), ((tm, d), F32)],
            resident=[((tm, d), F32)]),
        name="matmul_residual",
    )(y, w, h)


def _gmlp_mix_kernel(u_ref, v_ref, vg_ref, ws_ref, bias_ref, wout_ref, h_ref, o_ref, wsm_ref, y_ref,
                     *, n_chunks):
    @pl.when(pl.program_id(0) == 0)
    def _():
        row = lax.broadcasted_iota(jnp.int32, (CHUNK, CHUNK), 0)
        col = lax.broadcasted_iota(jnp.int32, (CHUNK, CHUNK), 1)
        for g in range(GMLP_GROUPS):
            wsm_ref[g] = jnp.where(col <= row, ws_ref[g], 0.0).astype(BF16)

    chunks_per_part = max(n_chunks // 2, 1)
    for c in range(n_chunks):
        rows = slice(c * CHUNK, (c + 1) * CHUNK)
        vn = _rms(v_ref[rows, :].astype(F32), vg_ref[...]).astype(BF16)
        for g in range(GMLP_GROUPS):
            cols = slice(g * LANES, (g + 1) * LANES)
            mixed = jnp.dot(wsm_ref[g], vn[:, cols], preferred_element_type=F32) + bias_ref[:, cols]
            y_ref[rows, cols] = (u_ref[rows, cols].astype(F32) * mixed).astype(BF16)
        if (c + 1) % chunks_per_part == 0:
            part = slice((c + 1 - chunks_per_part) * CHUNK, (c + 1) * CHUNK)
            o_ref[part, :] = h_ref[part, :] + jnp.dot(y_ref[part, :], wout_ref[...], preferred_element_type=F32)


def _gmlp_mix(z, v_gain, ws, bs, w_out, h, *, tm=512):
    t, d = h.shape
    tm = min(tm, t)
    width = z.shape[1] // 2
    assert width == GMLP_GROUPS * LANES and t % tm == 0 and tm % CHUNK == 0
    bias = jnp.repeat(bs.T, LANES, axis=1)
    return pl.pallas_call(
        functools.partial(_gmlp_mix_kernel, n_chunks=tm // CHUNK),
        grid=(t // tm,),
        in_specs=[
            pl.BlockSpec((tm, width), lambda i: (i, 0)),
            pl.BlockSpec((tm, width), lambda i: (i, 1)),
            pl.BlockSpec((1, width), lambda i: (0, 0)),
            pl.BlockSpec((GMLP_GROUPS, CHUNK, CHUNK), lambda i: (0, 0, 0)),
            pl.BlockSpec((CHUNK, width), lambda i: (0, 0)),
            pl.BlockSpec((width, d), lambda i: (0, 0)),
            pl.BlockSpec((tm, d), lambda i: (i, 0)),
        ],
        out_specs=pl.BlockSpec((tm, d), lambda i: (i, 0)),
        out_shape=jax.ShapeDtypeStruct((t, d), F32),
        scratch_shapes=[pltpu.VMEM((GMLP_GROUPS, CHUNK, CHUNK), BF16), pltpu.VMEM((tm, width), BF16)],
        compiler_params=_compiler_params(
            ("arbitrary",),
            pipelined=[((tm, width), BF16), ((tm, width), BF16), ((1, width), F32),
                       ((GMLP_GROUPS, CHUNK, CHUNK), F32), ((CHUNK, width), F32), ((width, d), BF16),
                       ((tm, d), F32), ((tm, d), F32)],
            resident=[((GMLP_GROUPS, CHUNK, CHUNK), BF16), ((tm, width), BF16), ((tm, d), F32)]),
        name="gmlp_mix",
    )(z, z, v_gain.reshape(1, width), ws, bias, w_out, h)


def _rope_tables(seq):
    half = ROPE_DIM // 2
    inv_freq = ROPE_THETA ** (-(jnp.arange(half, dtype=F32) * 2.0 / ROPE_DIM))
    ang = jnp.arange(seq, dtype=F32)[:, None] * inv_freq[None, :]
    cos, sin = jnp.cos(ang), jnp.sin(ang)
    rest = jnp.zeros((seq, SWA_HEAD_DIM - ROPE_DIM), F32)
    cos_h = jnp.concatenate([cos, cos, rest + 1.0], axis=1)
    sin_h = jnp.concatenate([sin, sin, rest], axis=1)
    rot = np.zeros((LANES, LANES), np.float32)
    for i in range(LANES):
        if i % SWA_HEAD_DIM < half:
            rot[i + half, i] = -1.0
        elif i % SWA_HEAD_DIM < ROPE_DIM:
            rot[i - half, i] = 1.0
    return (jnp.concatenate([cos_h, cos_h], axis=1), jnp.concatenate([sin_h, sin_h], axis=1),
            jnp.asarray(rot, BF16))


def _swa_kernel(sinks_ref, q_ref, kc_ref, kp_ref, vc_ref, vp_ref, cos_ref, sin_ref, rot_ref, o_ref, *, blocks):
    step = pl.program_id(1)
    rows = blocks * CHUNK
    in_current = (lax.broadcasted_iota(jnp.int32, (CHUNK, CHUNK), 1)
                  <= lax.broadcasted_iota(jnp.int32, (CHUNK, CHUNK), 0))
    log2e = np.float32(math.log2(math.e))
    cur = pl.ds(pl.multiple_of(step * rows, rows), rows)
    prev = pl.ds(pl.multiple_of(jnp.maximum(step * blocks - 1, 0) * CHUNK, CHUNK), CHUNK)
    low_lanes = lax.broadcasted_iota(jnp.int32, (CHUNK, LANES), 1) < SWA_HEAD_DIM
    pairs = SWA_Q_PER_KV // 2
    q_scale = np.float32(SWA_HEAD_DIM ** -0.5 * math.log2(math.e))

    n_q_blocks = SWA_KV_HEADS * pairs
    pieces = [q_ref[0, :, b * LANES:(b + 1) * LANES] for b in range(n_q_blocks)]
    pieces += [kc_ref[0, :, g * LANES:(g + 1) * LANES] for g in range(SWA_KV_HEADS)]
    pieces += [kp_ref[0, :, g * LANES:(g + 1) * LANES] for g in range(SWA_KV_HEADS)]
    partner = jnp.dot(jnp.concatenate(pieces, axis=0), rot_ref[...], preferred_element_type=F32)
    starts = np.cumsum([0] + [p.shape[0] for p in pieces])

    def rope(idx, table_rows):
        return (pieces[idx].astype(F32) * cos_ref[table_rows, :]
                + partner[starts[idx]:starts[idx + 1]] * sin_ref[table_rows, :])

    q_roped = [rope(b, cur) * q_scale for b in range(n_q_blocks)]
    k_own = [rope(n_q_blocks + g, cur).astype(BF16) for g in range(SWA_KV_HEADS)]
    k_before = [rope(n_q_blocks + SWA_KV_HEADS + g, prev).astype(BF16) for g in range(SWA_KV_HEADS)]

    def block_rows(x, blk):
        return x[blk * CHUNK:(blk + 1) * CHUNK]

    scores = {}
    for blk in range(blocks):
        for g in range(SWA_KV_HEADS):
            k_prev = k_before[g] if blk == 0 else block_rows(k_own[g], blk - 1)
            k = jnp.concatenate([k_prev, block_rows(k_own[g], blk)], axis=0)
            stacked = []
            for pair in range(pairs):
                q = block_rows(q_roped[g * pairs + pair], blk)
                stacked += [jnp.where(low_lanes, q, 0.0), jnp.where(low_lanes, 0.0, q)]
            q8 = jnp.concatenate(stacked, axis=0).astype(BF16)
            scores[blk, g] = lax.dot_general(q8, k, (((1,), (1,)), ((), ())), preferred_element_type=F32)

    for blk in range(blocks):
        for g in range(SWA_KV_HEADS):
            lanes_g = slice(g * LANES, (g + 1) * LANES)
            v_prev = vp_ref[0, :, lanes_g] if blk == 0 else vc_ref[0, (blk - 1) * CHUNK:blk * CHUNK, lanes_g]
            v = jnp.concatenate([v_prev, vc_ref[0, blk * CHUNK:(blk + 1) * CHUNK, lanes_g]], axis=0)
            s = scores[blk, g]
            probs, inv_denoms = [], []
            for h in range(SWA_Q_PER_KV):
                head_rows = slice(h * CHUNK, (h + 1) * CHUNK)
                s_prev = s[head_rows, :CHUNK]
                if blk == 0:
                    s_prev = jnp.where(step > 0, s_prev, NEG_INF)
                folded = jnp.where(in_current, s[head_rows, CHUNK:], s_prev)
                sink = sinks_ref[g * SWA_Q_PER_KV + h] * log2e
                m = jnp.maximum(jnp.max(folded, axis=1, keepdims=True), sink)
                p = jnp.exp2(folded - m)
                inv_denoms.append(1.0 / (jnp.sum(p, axis=1, keepdims=True) + jnp.exp2(sink - m)))
                probs.append(jnp.concatenate([jnp.where(in_current, 0.0, p), jnp.where(in_current, p, 0.0)],
                                             axis=1).astype(BF16))
            o8 = jnp.dot(jnp.concatenate(probs, axis=0), v, preferred_element_type=F32)
            for pair in range(pairs):
                lanes_q = slice((g * pairs + pair) * LANES, (g * pairs + pair + 1) * LANES)
                lo = o8[2 * pair * CHUNK:(2 * pair + 1) * CHUNK] * inv_denoms[2 * pair]
                hi = o8[(2 * pair + 1) * CHUNK:(2 * pair + 2) * CHUNK] * inv_denoms[2 * pair + 1]
                o_ref[0, blk * CHUNK:(blk + 1) * CHUNK, lanes_q] = jnp.where(low_lanes, lo, hi).astype(o_ref.dtype)


def _swa_attention(qkv, sinks, batch, seq, *, blocks=2):
    blocks = min(blocks, seq // CHUNK)
    rows = blocks * CHUNK
    assert seq % rows == 0
    qw = SWA_Q_HEADS * SWA_HEAD_DIM
    kw = SWA_KV_HEADS * LANES
    qkv = qkv.reshape(batch, seq, qw + 2 * kw)
    cos, sin, rot = _rope_tables(seq)
    k_blk = qw // kw
    table_spec = pl.BlockSpec((seq, LANES), lambda b, n: (0, 0))

    def before(n):
        return jnp.maximum(n * blocks - 1, 0)

    return pl.pallas_call(
        functools.partial(_swa_kernel, blocks=blocks),
        grid=(batch, seq // rows),
        in_specs=[
            pl.BlockSpec(memory_space=pltpu.SMEM),
            pl.BlockSpec((1, rows, qw), lambda b, n: (b, n, 0)),
            pl.BlockSpec((1, rows, kw), lambda b, n: (b, n, k_blk)),
            pl.BlockSpec((1, CHUNK, kw), lambda b, n: (b, before(n), k_blk)),
            pl.BlockSpec((1, rows, kw), lambda b, n: (b, n, k_blk + 1)),
            pl.BlockSpec((1, CHUNK, kw), lambda b, n: (b, before(n), k_blk + 1)),
            table_spec, table_spec, pl.BlockSpec((LANES, LANES), lambda b, n: (0, 0)),
        ],
        out_specs=pl.BlockSpec((1, rows, qw), lambda b, n: (b, n, 0)),
        out_shape=jax.ShapeDtypeStruct((batch, seq, qw), BF16),
        compiler_params=_compiler_params(
            ("parallel", "arbitrary"),
            pipelined=[((rows, qw), BF16)] * 2 + [((rows, kw), BF16)] * 2 + [((CHUNK, kw), BF16)] * 2
                      + [((seq, LANES), F32)] * 2,
            resident=[((SWA_Q_PER_KV * CHUNK, 2 * CHUNK), F32)] * (6 * blocks)),
        name="swa_attention",
    )(sinks, qkv, qkv, qkv, qkv, qkv, cos, sin, rot).reshape(batch * seq, qw)


FOX_CUMSUM_CHUNK = 256
FOX_Q_SCALE = FOX_HEAD_DIM ** -0.5 * math.log2(math.e)


def _fox_decay_kernel(fl_ref, bf_ref, o_ref, *, n_chunks):
    cc = FOX_CUMSUM_CHUNK
    row = lax.broadcasted_iota(jnp.int32, (cc, cc), 0)
    col = lax.broadcasted_iota(jnp.int32, (cc, cc), 1)
    lower = jnp.where(col <= row, 1.0, 0.0).astype(BF16)
    carry = jnp.zeros((1, LANES), F32)
    for c in range(n_chunks):
        rows = slice(c * cc, (c + 1) * cc)
        x = jax.nn.log_sigmoid(fl_ref[0, rows, :] + bf_ref[...])
        hi = x.astype(BF16)
        rest = x - hi.astype(F32)
        mid = rest.astype(BF16)
        lo = (rest - mid.astype(F32)).astype(BF16)
        cs = (jnp.dot(lower, hi, preferred_element_type=F32) + jnp.dot(lower, mid, preferred_element_type=F32)
              + jnp.dot(lower, lo, preferred_element_type=F32)) + carry
        o_ref[0, rows, :] = cs
        carry = cs[cc - 1:cc, :]


def _fox_decay(fl, b_f, batch, seq):
    assert seq % FOX_CUMSUM_CHUNK == 0
    bf = jnp.zeros((1, LANES), F32).at[0, :FOX_HEADS].set(b_f)
    return pl.pallas_call(
        functools.partial(_fox_decay_kernel, n_chunks=seq // FOX_CUMSUM_CHUNK),
        grid=(batch,),
        in_specs=[pl.BlockSpec((1, seq, LANES), lambda b: (b, 0, 0)), pl.BlockSpec((1, LANES), lambda b: (0, 0))],
        out_specs=pl.BlockSpec((1, seq, LANES), lambda b: (b, 0, 0)),
        out_shape=jax.ShapeDtypeStruct((batch, seq, LANES), F32),
        compiler_params=_compiler_params(
            ("parallel",), pipelined=[((seq, LANES), F32)] * 2, resident=[((seq, LANES), F32)]),
        name="fox_decay",
    )(fl.reshape(batch, seq, LANES), bf)


def _fox_kernel(q_ref, k_ref, v_ref, dq_ref, dk_ref, o_ref, *, tq, n_tiles, heads_per_step):
    log2e = np.float32(math.log2(math.e))
    lane = lax.broadcasted_iota(jnp.int32, (1, LANES), 1)
    row = lax.broadcasted_iota(jnp.int32, (tq, tq), 0)
    col = lax.broadcasted_iota(jnp.int32, (tq, tq), 1)
    causal = col <= row

    for local_head in range(heads_per_step):
        head_lane = lane == pl.program_id(1) * heads_per_step + local_head
        lanes = slice(local_head * FOX_HEAD_DIM, (local_head + 1) * FOX_HEAD_DIM)
        for i in range(n_tiles):
            rows = slice(i * tq, (i + 1) * tq)
            dq = jnp.sum(jnp.where(head_lane, dq_ref[0, rows, :], 0.0), axis=1, keepdims=True) * log2e
            q = q_ref[0, rows, lanes]
            m = l = acc = None
            for kt in range(i + 1):
                keys = slice(kt * tq, (kt + 1) * tq)
                s = lax.dot_general(q, k_ref[0, keys, lanes], (((1,), (1,)), ((), ())),
                                    preferred_element_type=F32)
                s = s + dq - dk_ref[0, local_head, kt:kt + 1, :] * log2e
                if kt == i:
                    s = jnp.where(causal, s, NEG_INF)
                m_tile = jnp.max(s, axis=1, keepdims=True)
                m_new = m_tile if m is None else jnp.maximum(m, m_tile)
                p = jnp.exp2(s - m_new)
                l_tile = jnp.sum(p, axis=1, keepdims=True)
                pv = jnp.dot(p.astype(BF16), v_ref[0, keys, lanes], preferred_element_type=F32)
                if m is None:
                    l, acc = l_tile, pv
                else:
                    alpha = jnp.exp2(m - m_new)
                    l, acc = alpha * l + l_tile, alpha * acc + pv
                m = m_new
            o_ref[0, rows, lanes] = (acc / l).astype(o_ref.dtype)


def _fox_attention(qkv, dec, batch, seq, *, tq=512, heads_per_step=2):
    width = FOX_HEADS * FOX_HEAD_DIM
    tq = min(tq, seq)
    assert seq % tq == 0 and FOX_HEADS % heads_per_step == 0
    head_groups = FOX_HEADS // heads_per_step
    group_width = heads_per_step * FOX_HEAD_DIM
    qkv = qkv.reshape(batch, seq, 3 * width)
    dec_k = dec[:, :, :FOX_HEADS].transpose(0, 2, 1).reshape(batch, FOX_HEADS, seq // tq, tq)

    def group_spec(first_block):
        return pl.BlockSpec((1, seq, group_width), lambda b, h: (b, 0, first_block + h))

    return pl.pallas_call(
        functools.partial(_fox_kernel, tq=tq, n_tiles=seq // tq, heads_per_step=heads_per_step),
        grid=(batch, head_groups),
        in_specs=[
            group_spec(0), group_spec(head_groups), group_spec(2 * head_groups),
            pl.BlockSpec((1, seq, LANES), lambda b, h: (b, 0, 0)),
            pl.BlockSpec((1, heads_per_step, seq // tq, tq), lambda b, h: (b, h, 0, 0)),
        ],
        out_specs=group_spec(0),
        out_shape=jax.ShapeDtypeStruct((batch, seq, width), BF16),
        compiler_params=_compiler_params(
            ("parallel", "parallel"),
            pipelined=[((seq, group_width), BF16)] * 4 + [((seq, LANES), F32), ((FOX_HEADS, tq), F32)],
            resident=[((tq, tq), F32)] * 12),
        name="fox_attention",
    )(qkv, qkv, qkv, dec, dec_k).reshape(batch * seq, width)


def _gmlp_layer(h, w_in, v_gain, ws, bs, w_out):
    z = _norm_matmul(h, w_in, gelu=True)
    return _gmlp_mix(z, v_gain, ws, bs, w_out, h)


def _swa_layer(h, w_in, sinks, w_out, batch, seq):
    d = h.shape[1]
    qw = SWA_Q_HEADS * SWA_HEAD_DIM
    kvw = SWA_KV_HEADS * SWA_HEAD_DIM

    def duplicate_heads(w):
        w = w.reshape(d, SWA_KV_HEADS, 1, SWA_HEAD_DIM)
        return jnp.broadcast_to(w, (d, SWA_KV_HEADS, 2, SWA_HEAD_DIM)).reshape(d, SWA_KV_HEADS * LANES)

    w = jnp.concatenate([w_in[:, :qw], duplicate_heads(w_in[:, qw:qw + kvw]),
                         duplicate_heads(w_in[:, qw + kvw:])], axis=1)
    qkv = _norm_matmul(h, w)
    o = _swa_attention(qkv, sinks, batch, seq)
    return _matmul_residual(o, w_out, h)


def _fox_layer(h, w_in, b_f, w_out, batch, seq):
    d = h.shape[1]
    width = FOX_HEADS * FOX_HEAD_DIM
    w_f = jnp.zeros((d, LANES), BF16).at[:, :FOX_HEADS].set(w_in[:, 3 * width:])
    qkv, fl = _norm_matmul(h, w_in, w_f, n_cols=3 * width, scaled_cols=width, scale=FOX_Q_SCALE)
    dec = _fox_decay(fl, b_f, batch, seq)
    o = _fox_attention(qkv, dec, batch, seq)
    return _matmul_residual(o, w_out, h)


def kernel(x, l0_ffn1_norm, l0_ffn1_wi, l0_ffn1_wo, l0_mix_norm, l0_mix_win, l0_gmlp_vnorm, l0_gmlp_ws, l0_gmlp_bs, l0_mix_wout, l0_ffn2_norm, l0_ffn2_wi, l0_ffn2_wo, l1_ffn1_norm, l1_ffn1_wi, l1_ffn1_wo, l1_mix_norm, l1_mix_win, l1_swa_sinks, l1_mix_wout, l1_ffn2_norm, l1_ffn2_wi, l1_ffn2_wo, l2_ffn1_norm, l2_ffn1_wi, l2_ffn1_wo, l2_mix_norm, l2_mix_win, l2_fox_bf, l2_mix_wout, l2_ffn2_norm, l2_ffn2_wi, l2_ffn2_wo, l3_ffn1_norm, l3_ffn1_wi, l3_ffn1_wo, l3_mix_norm, l3_mix_win, l3_gmlp_vnorm, l3_gmlp_ws, l3_gmlp_bs, l3_mix_wout, l3_ffn2_norm, l3_ffn2_wi, l3_ffn2_wo, final_norm):
    batch, seq, d = x.shape
    h = x.reshape(batch * seq, d)

    ffn_weights = [(l0_ffn1_norm, l0_ffn1_wi, l0_ffn1_wo), (l0_ffn2_norm, l0_ffn2_wi, l0_ffn2_wo),
                   (l1_ffn1_norm, l1_ffn1_wi, l1_ffn1_wo), (l1_ffn2_norm, l1_ffn2_wi, l1_ffn2_wo),
                   (l2_ffn1_norm, l2_ffn1_wi, l2_ffn1_wo), (l2_ffn2_norm, l2_ffn2_wi, l2_ffn2_wo),
                   (l3_ffn1_norm, l3_ffn1_wi, l3_ffn1_wo), (l3_ffn2_norm, l3_ffn2_wi, l3_ffn2_wo)]
    gain0, wi0, wo0 = ffn_weights[0]
    cast = [((gain0[:, None] * wi0).astype(BF16), (0.5 * wo0).astype(BF16))]

    def ffn(h, final_gain=None, cast_extra=()):
        index = len(cast) - 1
        wi, wo_half = cast[index]
        next_weights = ffn_weights[index + 1] if index + 1 < len(ffn_weights) else None
        h, cast_next, extra = _ffn(h, wi, wo_half, final_gain, next_weights, cast_extra)
        cast.append(cast_next)
        return h, extra

    h, (w_in, w_out) = ffn(h, cast_extra=((l0_mix_win, l0_mix_norm), (l0_mix_wout, None)))
    h = _gmlp_layer(h, w_in, l0_gmlp_vnorm, l0_gmlp_ws, l0_gmlp_bs, w_out)
    h, _ = ffn(h)

    h, (w_in, w_out) = ffn(h, cast_extra=((l1_mix_win, l1_mix_norm), (l1_mix_wout, None)))
    h = _swa_layer(h, w_in, l1_swa_sinks, w_out, batch, seq)
    h, _ = ffn(h)

    h, (w_in, w_out) = ffn(h, cast_extra=((l2_mix_win, l2_mix_norm), (l2_mix_wout, None)))
    h = _fox_layer(h, w_in, l2_fox_bf, w_out, batch, seq)
    h, _ = ffn(h)

    h, (w_in, w_out) = ffn(h, cast_extra=((l3_mix_win, l3_mix_norm), (l3_mix_wout, None)))
    h = _gmlp_layer(h, w_in, l3_gmlp_vnorm, l3_gmlp_ws, l3_gmlp_bs, w_out)
    h, _ = ffn(h, final_norm)
    return h.reshape(batch, seq, d)
```

```python
import functools
import math

import jax
import jax.numpy as jnp
import numpy as np
from jax import lax
from jax.experimental import pallas as pl
from jax.experimental.pallas import tpu as pltpu

F32 = jnp.float32
BF16 = jnp.bfloat16

NORM_EPS = 1e-5
NEG_INF = -1e30
CHUNK = 128
GMLP_GROUPS = 16
SWA_HEAD_DIM = 64
SWA_Q_HEADS = 32
SWA_KV_HEADS = 4
SWA_Q_PER_KV = SWA_Q_HEADS // SWA_KV_HEADS
ROPE_THETA = 500000.0
ROPE_DIM = SWA_HEAD_DIM // 4
FOX_HEAD_DIM = 128
FOX_HEADS = 16

LANES = 128
V7X_VMEM_BYTES = 64 * 1024 * 1024
V7X_VMEM_USABLE_BYTES = V7X_VMEM_BYTES - 4 * 1024 * 1024
V7X_DEFAULT_SCOPED_VMEM_BYTES = 32 * 1024 * 1024
SPILL_ALLOWANCE_DIVISOR = 4


def _nbytes(shape, dtype):
    return math.prod(shape) * jnp.dtype(dtype).itemsize


def _compiler_params(semantics, pipelined, resident):
    need = 2 * sum(_nbytes(s, d) for s, d in pipelined) + sum(_nbytes(s, d) for s, d in resident)
    need += need // SPILL_ALLOWANCE_DIVISOR
    limit = min(V7X_VMEM_USABLE_BYTES, max(V7X_DEFAULT_SCOPED_VMEM_BYTES, need))
    return pltpu.CompilerParams(dimension_semantics=semantics, vmem_limit_bytes=limit)


def _rms(x, gain):
    return x * lax.rsqrt(jnp.mean(x * x, axis=-1, keepdims=True) + NORM_EPS) * gain


def _rms_prepare(x, rinv_ref, xb_ref):
    rinv = lax.rsqrt(jnp.mean(x * x, axis=-1, keepdims=True) + NORM_EPS)
    rinv_ref[...] = jnp.broadcast_to(rinv, rinv_ref.shape)
    xb_ref[...] = x.astype(BF16)


def _row_scale(y, rinv):
    return y * jnp.concatenate([rinv] * (y.shape[1] // LANES), axis=1)


def _ffn_kernel(*refs, n_ff_tiles, final_norm, cast_next, cast_extra_gains):
    refs = list(refs)
    x_ref, wg_ref, wu_ref, wo_ref = refs[:4]
    del refs[:4]
    gf_ref = refs.pop(0) if final_norm else None
    if cast_next:
        wi_next_ref, wi_next_gain_ref, wo_next_ref = refs[:3]
        del refs[:3]
    extra_refs = []
    for has_gain in cast_extra_gains:
        extra_refs.append((refs.pop(0), refs.pop(0) if has_gain else None))
    o_ref = refs.pop(0)
    if cast_next:
        wi_next_bf16_ref, wo_next_bf16_ref = refs[:2]
        del refs[:2]
    extra_bf16_refs = refs[:len(cast_extra_gains)]
    del refs[:len(cast_extra_gains)]
    xb_ref, rinv_ref = refs
    j = pl.program_id(1)

    def step(first):
        if first:
            _rms_prepare(x_ref[...], rinv_ref, xb_ref)
        xb = xb_ref[...]
        rinv = rinv_ref[...]
        gate = _row_scale(jnp.dot(xb, wg_ref[...], preferred_element_type=F32), rinv)
        up = _row_scale(jnp.dot(xb, wu_ref[...], preferred_element_type=F32), rinv)
        act = (gate * jax.nn.sigmoid(gate) * up).astype(BF16)
        update = jnp.dot(act, wo_ref[...], preferred_element_type=F32)
        o_ref[...] = (x_ref[...] if first else o_ref[...]) + update

        if cast_next:
            wi_next_bf16_ref[...] = (wi_next_gain_ref[...] * wi_next_ref[...]).astype(BF16)
            wo_next_bf16_ref[...] = (0.5 * wo_next_ref[...]).astype(BF16)
        for (src_ref, gain_ref), dst_ref in zip(extra_refs, extra_bf16_refs):
            w = src_ref[...]
            dst_ref[...] = (w if gain_ref is None else gain_ref[...] * w).astype(BF16)

    pl.when(j == 0)(functools.partial(step, True))
    pl.when(j > 0)(functools.partial(step, False))

    if final_norm:
        @pl.when(j == n_ff_tiles - 1)
        def _():
            o_ref[...] = _rms(o_ref[...], gf_ref[...])


BF16_SUBLANES = 16


def _ffn(h, wi, wo_half, final_gain=None, next_weights=None, cast_extra=(), *, tm=1024, tf=512):
    wo = wo_half
    t, d = h.shape
    tm = min(tm, t)
    d_ff = wo.shape[0]
    n_ff_tiles = d_ff // tf
    n_row_tiles = t // tm
    assert t % tm == 0 and d_ff % tf == 0
    in_specs = [
        pl.BlockSpec((tm, d), lambda i, j: (i, 0)),
        pl.BlockSpec((d, tf), lambda i, j: (0, j)),
        pl.BlockSpec((d, tf), lambda i, j: (0, j + n_ff_tiles)),
        pl.BlockSpec((tf, d), lambda i, j: (j, 0)),
    ]
    args = [h, wi, wi, wo]
    out_specs = [pl.BlockSpec((tm, d), lambda i, j: (i, 0))]
    out_shape = [jax.ShapeDtypeStruct((t, d), F32)]
    pipelined = [((tm, d), F32), ((tm, d), F32), ((d, tf), BF16), ((d, tf), BF16), ((tf, d), BF16)]
    if final_gain is not None:
        in_specs.append(pl.BlockSpec((1, d), lambda i, j: (0, 0)))
        args.append(final_gain.reshape(1, d))
    if next_weights is not None:
        gain_next, wi_next, wo_next = next_weights
        assert wi_next.shape == wi.shape and wo_next.shape == wo.shape
        assert d % n_row_tiles == 0 and wi.shape[1] % n_ff_tiles == 0
        wi_block = (d // n_row_tiles, wi.shape[1] // n_ff_tiles)
        wo_block = (d_ff // n_ff_tiles, d // n_row_tiles)
        wi_spec = pl.BlockSpec(wi_block, lambda i, j: (i, j))
        wo_spec = pl.BlockSpec(wo_block, lambda i, j: (j, i))
        in_specs += [wi_spec, pl.BlockSpec((wi_block[0], 1), lambda i, j: (i, 0)), wo_spec]
        args += [wi_next, gain_next.reshape(d, 1), wo_next]
        out_specs += [wi_spec, wo_spec]
        out_shape += [jax.ShapeDtypeStruct(wi.shape, BF16), jax.ShapeDtypeStruct(wo.shape, BF16)]
        pipelined += [(wi_block, F32), (wo_block, F32), (wi_block, BF16), (wo_block, BF16), ((wi_block[0], LANES), F32)]
    extra_out_specs, extra_out_shape = [], []
    for w, row_gain in cast_extra:
        rows, cols = w.shape
        n_sub = max(s for s in range(1, n_ff_tiles + 1)
                    if rows % (n_row_tiles * s * BF16_SUBLANES) == 0)
        block_rows = rows // (n_row_tiles * n_sub)

        def row_block(i, j, n_sub=n_sub):
            return (i * n_sub + jnp.minimum(j, n_sub - 1), 0)

        spec = pl.BlockSpec((block_rows, cols), row_block)
        in_specs.append(spec)
        args.append(w)
        if row_gain is not None:
            in_specs.append(pl.BlockSpec((block_rows, 1), row_block))
            args.append(row_gain.reshape(rows, 1))
            pipelined.append(((block_rows, LANES), F32))
        extra_out_specs.append(spec)
        extra_out_shape.append(jax.ShapeDtypeStruct(w.shape, BF16))
        pipelined += [((block_rows, cols), F32), ((block_rows, cols), BF16)]
    out_specs += extra_out_specs
    out_shape += extra_out_shape
    outs = pl.pallas_call(
        functools.partial(_ffn_kernel, n_ff_tiles=n_ff_tiles, final_norm=final_gain is not None,
                          cast_next=next_weights is not None,
                          cast_extra_gains=tuple(g is not None for _, g in cast_extra)),
        grid=(n_row_tiles, n_ff_tiles),
        in_specs=in_specs,
        out_specs=out_specs,
        out_shape=out_shape,
        scratch_shapes=[pltpu.VMEM((tm, d), BF16), pltpu.VMEM((tm, LANES), F32)],
        compiler_params=_compiler_params(
            ("parallel", "arbitrary"), pipelined=pipelined,
            resident=[((tm, d), BF16), ((tm, LANES), F32), ((tm, tf), F32), ((tm, tf), F32), ((tm, tf), F32)]),
        name="ffn",
    )(*args)
    n_next = 0 if next_weights is None else 2
    return outs[0], (tuple(outs[1:3]) if n_next else None), tuple(outs[1 + n_next:])


def _norm_matmul_kernel(*refs, gelu, side, scaled_tiles, scale):
    if side:
        x_ref, w_ref, w2_ref, o_ref, o2_ref, xb_ref, rinv_ref = refs
    else:
        x_ref, w_ref, o_ref, xb_ref, rinv_ref = refs
    j = pl.program_id(1)

    def step(first):
        if first:
            _rms_prepare(x_ref[...], rinv_ref, xb_ref)
            if side:
                o2_ref[...] = _row_scale(jnp.dot(xb_ref[...], w2_ref[...], preferred_element_type=F32),
                                         rinv_ref[...])
        y = _row_scale(jnp.dot(xb_ref[...], w_ref[...], preferred_element_type=F32), rinv_ref[...])
        if gelu:
            y = 0.5 * y * (1.0 + lax.erf(y * np.float32(math.sqrt(0.5))))
        if scaled_tiles:
            y = y * jnp.where(j < scaled_tiles, np.float32(scale), np.float32(1.0))
        o_ref[...] = y.astype(o_ref.dtype)

    pl.when(j == 0)(functools.partial(step, True))
    pl.when(j > 0)(functools.partial(step, False))


def _norm_matmul(h, w, w_side=None, *, n_cols=None, gelu=False, scaled_cols=0, scale=1.0, tm=1024, tn=1024):
    t, d = h.shape
    tm = min(tm, t)
    n = w.shape[1] if n_cols is None else n_cols
    assert t % tm == 0 and n % tn == 0 and n <= w.shape[1] and scaled_cols % tn == 0
    side = w_side is not None
    in_specs = [
        pl.BlockSpec((tm, d), lambda i, j: (i, 0)),
        pl.BlockSpec((d, tn), lambda i, j: (0, j)),
    ]
    args = [h, w]
    out_specs = pl.BlockSpec((tm, tn), lambda i, j: (i, j))
    out_shape = jax.ShapeDtypeStruct((t, n), BF16)
    pipelined = [((tm, d), F32), ((d, tn), BF16), ((tm, tn), BF16)]
    if side:
        n2 = w_side.shape[1]
        assert n2 == LANES
        in_specs.append(pl.BlockSpec((d, n2), lambda i, j: (0, 0)))
        args.append(w_side)
        out_specs = [out_specs, pl.BlockSpec((tm, n2), lambda i, j: (i, 0))]
        out_shape = [out_shape, jax.ShapeDtypeStruct((t, n2), F32)]
        pipelined += [((d, n2), BF16), ((tm, n2), F32)]
    return pl.pallas_call(
        functools.partial(_norm_matmul_kernel, gelu=gelu, side=side, scaled_tiles=scaled_cols // tn, scale=scale),
        grid=(t // tm, n // tn),
        in_specs=in_specs,
        out_specs=out_specs,
        out_shape=out_shape,
        scratch_shapes=[pltpu.VMEM((tm, d), BF16), pltpu.VMEM((tm, LANES), F32)],
        compiler_params=_compiler_params(
            ("parallel", "arbitrary"), pipelined=pipelined,
            resident=[((tm, d), BF16), ((tm, LANES), F32), ((tm, tn), F32), ((tm, tn), F32)]),
        name="norm_matmul",
    )(*args)


def _matmul_residual_kernel(y_ref, w_ref, h_ref, o_ref):
    o_ref[...] = h_ref[...] + jnp.dot(y_ref[...], w_ref[...], preferred_element_type=F32)


def _matmul_residual(y, w, h, *, tm=512):
    t, k = y.shape
    d = w.shape[1]
    assert t % tm == 0
    return pl.pallas_call(
        _matmul_residual_kernel,
        grid=(t // tm,),
        in_specs=[
            pl.BlockSpec((tm, k), lambda i: (i, 0)),
            pl.BlockSpec((k, d), lambda i: (0, 0)),
            pl.BlockSpec((tm, d), lambda i: (i, 0)),
        ],
        out_specs=pl.BlockSpec((tm, d), lambda i: (i, 0)),
        out_shape=jax.ShapeDtypeStruct((t, d), F32),
        compiler_params=_compiler_params(
            ("parallel",),
            pipelined=[((tm, k), BF16), ((k, d), BF16), ((tm, d), F32), ((tm, d), F32)],
            resident=[((tm, d), F32)]),
        name="matmul_residual",
    )(y, w, h)


def _gmlp_mix_kernel(u_ref, v_ref, vg_ref, ws_ref, bias_ref, wout_ref, h_ref, o_ref, wsm_ref, y_ref,
                     *, n_chunks):
    @pl.when(pl.program_id(0) == 0)
    def _():
        row = lax.broadcasted_iota(jnp.int32, (CHUNK, CHUNK), 0)
        col = lax.broadcasted_iota(jnp.int32, (CHUNK, CHUNK), 1)
        for g in range(GMLP_GROUPS):
            wsm_ref[g] = jnp.where(col <= row, ws_ref[g], 0.0).astype(BF16)

    chunks_per_part = max(n_chunks // 2, 1)
    for c in range(n_chunks):
        rows = slice(c * CHUNK, (c + 1) * CHUNK)
        vn = _rms(v_ref[rows, :].astype(F32), vg_ref[...]).astype(BF16)
        for g in range(GMLP_GROUPS):
            cols = slice(g * LANES, (g + 1) * LANES)
            mixed = jnp.dot(wsm_ref[g], vn[:, cols], preferred_element_type=F32) + bias_ref[:, cols]
            y_ref[rows, cols] = (u_ref[rows, cols].astype(F32) * mixed).astype(BF16)
        if (c + 1) % chunks_per_part == 0:
            part = slice((c + 1 - chunks_per_part) * CHUNK, (c + 1) * CHUNK)
            o_ref[part, :] = h_ref[part, :] + jnp.dot(y_ref[part, :], wout_ref[...], preferred_element_type=F32)


def _gmlp_mix(z, v_gain, ws, bs, w_out, h, *, tm=512):
    t, d = h.shape
    tm = min(tm, t)
    width = z.shape[1] // 2
    assert width == GMLP_GROUPS * LANES and t % tm == 0 and tm % CHUNK == 0
    bias = jnp.repeat(bs.T, LANES, axis=1)
    return pl.pallas_call(
        functools.partial(_gmlp_mix_kernel, n_chunks=tm // CHUNK),
        grid=(t // tm,),
        in_specs=[
            pl.BlockSpec((tm, width), lambda i: (i, 0)),
            pl.BlockSpec((tm, width), lambda i: (i, 1)),
            pl.BlockSpec((1, width), lambda i: (0, 0)),
            pl.BlockSpec((GMLP_GROUPS, CHUNK, CHUNK), lambda i: (0, 0, 0)),
            pl.BlockSpec((CHUNK, width), lambda i: (0, 0)),
            pl.BlockSpec((width, d), lambda i: (0, 0)),
            pl.BlockSpec((tm, d), lambda i: (i, 0)),
        ],
        out_specs=pl.BlockSpec((tm, d), lambda i: (i, 0)),
        out_shape=jax.ShapeDtypeStruct((t, d), F32),
        scratch_shapes=[pltpu.VMEM((GMLP_GROUPS, CHUNK, CHUNK), BF16), pltpu.VMEM((tm, width), BF16)],
        compiler_params=_compiler_params(
            ("arbitrary",),
            pipelined=[((tm, width), BF16), ((tm, width), BF16), ((1, width), F32),
                       ((GMLP_GROUPS, CHUNK, CHUNK), F32), ((CHUNK, width), F32), ((width, d), BF16),
                       ((tm, d), F32), ((tm, d), F32)],
            resident=[((GMLP_GROUPS, CHUNK, CHUNK), BF16), ((tm, width), BF16), ((tm, d), F32)]),
        name="gmlp_mix",
    )(z, z, v_gain.reshape(1, width), ws, bias, w_out, h)


def _rope_tables(seq):
    half = ROPE_DIM // 2
    inv_freq = ROPE_THETA ** (-(jnp.arange(half, dtype=F32) * 2.0 / ROPE_DIM))
    ang = jnp.arange(seq, dtype=F32)[:, None] * inv_freq[None, :]
    cos, sin = jnp.cos(ang), jnp.sin(ang)
    rest = jnp.zeros((seq, SWA_HEAD_DIM - ROPE_DIM), F32)
    cos_h = jnp.concatenate([cos, cos, rest + 1.0], axis=1)
    sin_h = jnp.concatenate([sin, sin, rest], axis=1)
    rot = np.zeros((LANES, LANES), np.float32)
    for i in range(LANES):
        if i % SWA_HEAD_DIM < half:
            rot[i + half, i] = -1.0
        elif i % SWA_HEAD_DIM < ROPE_DIM:
            rot[i - half, i] = 1.0
    return (jnp.concatenate([cos_h, cos_h], axis=1), jnp.concatenate([sin_h, sin_h], axis=1),
            jnp.asarray(rot, BF16))


def _swa_kernel(sinks_ref, q_ref, kc_ref, kp_ref, vc_ref, vp_ref, cos_ref, sin_ref, rot_ref, o_ref, *, blocks):
    step = pl.program_id(1)
    rows = blocks * CHUNK
    in_current = (lax.broadcasted_iota(jnp.int32, (CHUNK, CHUNK), 1)
                  <= lax.broadcasted_iota(jnp.int32, (CHUNK, CHUNK), 0))
    log2e = np.float32(math.log2(math.e))
    cur = pl.ds(pl.multiple_of(step * rows, rows), rows)
    prev = pl.ds(pl.multiple_of(jnp.maximum(step * blocks - 1, 0) * CHUNK, CHUNK), CHUNK)
    low_lanes = lax.broadcasted_iota(jnp.int32, (CHUNK, LANES), 1) < SWA_HEAD_DIM
    pairs = SWA_Q_PER_KV // 2
    q_scale = np.float32(SWA_HEAD_DIM ** -0.5 * math.log2(math.e))

    n_q_blocks = SWA_KV_HEADS * pairs
    pieces = [q_ref[0, :, b * LANES:(b + 1) * LANES] for b in range(n_q_blocks)]
    pieces += [kc_ref[0, :, g * LANES:(g + 1) * LANES] for g in range(SWA_KV_HEADS)]
    pieces += [kp_ref[0, :, g * LANES:(g + 1) * LANES] for g in range(SWA_KV_HEADS)]
    partner = jnp.dot(jnp.concatenate(pieces, axis=0), rot_ref[...], preferred_element_type=F32)
    starts = np.cumsum([0] + [p.shape[0] for p in pieces])

    def rope(idx, table_rows):
        return (pieces[idx].astype(F32) * cos_ref[table_rows, :]
                + partner[starts[idx]:starts[idx + 1]] * sin_ref[table_rows, :])

    q_roped = [rope(b, cur) * q_scale for b in range(n_q_blocks)]
    k_own = [rope(n_q_blocks + g, cur).astype(BF16) for g in range(SWA_KV_HEADS)]
    k_before = [rope(n_q_blocks + SWA_KV_HEADS + g, prev).astype(BF16) for g in range(SWA_KV_HEADS)]

    def block_rows(x, blk):
        return x[blk * CHUNK:(blk + 1) * CHUNK]

    scores = {}
    for blk in range(blocks):
        for g in range(SWA_KV_HEADS):
            k_prev = k_before[g] if blk == 0 else block_rows(k_own[g], blk - 1)
            k = jnp.concatenate([k_prev, block_rows(k_own[g], blk)], axis=0)
            stacked = []
            for pair in range(pairs):
                q = block_rows(q_roped[g * pairs + pair], blk)
                stacked += [jnp.where(low_lanes, q, 0.0), jnp.where(low_lanes, 0.0, q)]
            q8 = jnp.concatenate(stacked, axis=0).astype(BF16)
            scores[blk, g] = lax.dot_general(q8, k, (((1,), (1,)), ((), ())), preferred_element_type=F32)

    for blk in range(blocks):
        for g in range(SWA_KV_HEADS):
            lanes_g = slice(g * LANES, (g + 1) * LANES)
            v_prev = vp_ref[0, :, lanes_g] if blk == 0 else vc_ref[0, (blk - 1) * CHUNK:blk * CHUNK, lanes_g]
            v = jnp.concatenate([v_prev, vc_ref[0, blk * CHUNK:(blk + 1) * CHUNK, lanes_g]], axis=0)
            s = scores[blk, g]
            probs, inv_denoms = [], []
            for h in range(SWA_Q_PER_KV):
                head_rows = slice(h * CHUNK, (h + 1) * CHUNK)
                s_prev = s[head_rows, :CHUNK]
                if blk == 0:
                    s_prev = jnp.where(step > 0, s_prev, NEG_INF)
                folded = jnp.where(in_current, s[head_rows, CHUNK:], s_prev)
                sink = sinks_ref[g * SWA_Q_PER_KV + h] * log2e
                m = jnp.maximum(jnp.max(folded, axis=1, keepdims=True), sink)
                p = jnp.exp2(folded - m)
                inv_denoms.append(1.0 / (jnp.sum(p, axis=1, keepdims=True) + jnp.exp2(sink - m)))
                probs.append(jnp.concatenate([jnp.where(in_current, 0.0, p), jnp.where(in_current, p, 0.0)],
                                             axis=1).astype(BF16))
            o8 = jnp.dot(jnp.concatenate(probs, axis=0), v, preferred_element_type=F32)
            for pair in range(pairs):
                lanes_q = slice((g * pairs + pair) * LANES, (g * pairs + pair + 1) * LANES)
                lo = o8[2 * pair * CHUNK:(2 * pair + 1) * CHUNK] * inv_denoms[2 * pair]
                hi = o8[(2 * pair + 1) * CHUNK:(2 * pair + 2) * CHUNK] * inv_denoms[2 * pair + 1]
                o_ref[0, blk * CHUNK:(blk + 1) * CHUNK, lanes_q] = jnp.where(low_lanes, lo, hi).astype(o_ref.dtype)


def _swa_attention(qkv, sinks, batch, seq, *, blocks=4):
    blocks = min(blocks, seq // CHUNK)
    rows = blocks * CHUNK
    assert seq % rows == 0
    qw = SWA_Q_HEADS * SWA_HEAD_DIM
    kw = SWA_KV_HEADS * LANES
    qkv = qkv.reshape(batch, seq, qw + 2 * kw)
    cos, sin, rot = _rope_tables(seq)
    k_blk = qw // kw
    table_spec = pl.BlockSpec((seq, LANES), lambda b, n: (0, 0))

    def before(n):
        return jnp.maximum(n * blocks - 1, 0)

    return pl.pallas_call(
        functools.partial(_swa_kernel, blocks=blocks),
        grid=(batch, seq // rows),
        in_specs=[
            pl.BlockSpec(memory_space=pltpu.SMEM),
            pl.BlockSpec((1, rows, qw), lambda b, n: (b, n, 0)),
            pl.BlockSpec((1, rows, kw), lambda b, n: (b, n, k_blk)),
            pl.BlockSpec((1, CHUNK, kw), lambda b, n: (b, before(n), k_blk)),
            pl.BlockSpec((1, rows, kw), lambda b, n: (b, n, k_blk + 1)),
            pl.BlockSpec((1, CHUNK, kw), lambda b, n: (b, before(n), k_blk + 1)),
            table_spec, table_spec, pl.BlockSpec((LANES, LANES), lambda b, n: (0, 0)),
        ],
        out_specs=pl.BlockSpec((1, rows, qw), lambda b, n: (b, n, 0)),
        out_shape=jax.ShapeDtypeStruct((batch, seq, qw), BF16),
        compiler_params=_compiler_params(
            ("parallel", "arbitrary"),
            pipelined=[((rows, qw), BF16)] * 2 + [((rows, kw), BF16)] * 2 + [((CHUNK, kw), BF16)] * 2
                      + [((seq, LANES), F32)] * 2,
            resident=[((SWA_Q_PER_KV * CHUNK, 2 * CHUNK), F32)] * (6 * blocks)),
        name="swa_attention",
    )(sinks, qkv, qkv, qkv, qkv, qkv, cos, sin, rot).reshape(batch * seq, qw)


FOX_CUMSUM_CHUNK = 256
FOX_Q_SCALE = FOX_HEAD_DIM ** -0.5 * math.log2(math.e)


def _fox_decay_kernel(fl_ref, bf_ref, o_ref, *, n_chunks):
    cc = FOX_CUMSUM_CHUNK
    row = lax.broadcasted_iota(jnp.int32, (cc, cc), 0)
    col = lax.broadcasted_iota(jnp.int32, (cc, cc), 1)
    lower = jnp.where(col <= row, 1.0, 0.0).astype(BF16)
    carry = jnp.zeros((1, LANES), F32)
    for c in range(n_chunks):
        rows = slice(c * cc, (c + 1) * cc)
        x = jax.nn.log_sigmoid(fl_ref[0, rows, :] + bf_ref[...])
        hi = x.astype(BF16)
        rest = x - hi.astype(F32)
        mid = rest.astype(BF16)
        lo = (rest - mid.astype(F32)).astype(BF16)
        cs = (jnp.dot(lower, hi, preferred_element_type=F32) + jnp.dot(lower, mid, preferred_element_type=F32)
              + jnp.dot(lower, lo, preferred_element_type=F32)) + carry
        o_ref[0, rows, :] = cs
        carry = cs[cc - 1:cc, :]


def _fox_decay(fl, b_f, batch, seq):
    assert seq % FOX_CUMSUM_CHUNK == 0
    bf = jnp.zeros((1, LANES), F32).at[0, :FOX_HEADS].set(b_f)
    return pl.pallas_call(
        functools.partial(_fox_decay_kernel, n_chunks=seq // FOX_CUMSUM_CHUNK),
        grid=(batch,),
        in_specs=[pl.BlockSpec((1, seq, LANES), lambda b: (b, 0, 0)), pl.BlockSpec((1, LANES), lambda b: (0, 0))],
        out_specs=pl.BlockSpec((1, seq, LANES), lambda b: (b, 0, 0)),
        out_shape=jax.ShapeDtypeStruct((batch, seq, LANES), F32),
        compiler_params=_compiler_params(
            ("parallel",), pipelined=[((seq, LANES), F32)] * 2, resident=[((seq, LANES), F32)]),
        name="fox_decay",
    )(fl.reshape(batch, seq, LANES), bf)


def _fox_kernel(q_ref, k_ref, v_ref, dq_ref, dk_ref, o_ref, *, tq, n_tiles, heads_per_step):
    log2e = np.float32(math.log2(math.e))
    lane = lax.broadcasted_iota(jnp.int32, (1, LANES), 1)
    row = lax.broadcasted_iota(jnp.int32, (tq, tq), 0)
    col = lax.broadcasted_iota(jnp.int32, (tq, tq), 1)
    causal = col <= row

    for local_head in range(heads_per_step):
        head_lane = lane == pl.program_id(1) * heads_per_step + local_head
        lanes = slice(local_head * FOX_HEAD_DIM, (local_head + 1) * FOX_HEAD_DIM)
        for i in range(n_tiles):
            rows = slice(i * tq, (i + 1) * tq)
            dq = jnp.sum(jnp.where(head_lane, dq_ref[0, rows, :], 0.0), axis=1, keepdims=True) * log2e
            q = q_ref[0, rows, lanes]
            m = l = acc = None
            for kt in range(i + 1):
                keys = slice(kt * tq, (kt + 1) * tq)
                s = lax.dot_general(q, k_ref[0, keys, lanes], (((1,), (1,)), ((), ())),
                                    preferred_element_type=F32)
                s = s + dq - dk_ref[0, local_head, kt:kt + 1, :] * log2e
                if kt == i:
                    s = jnp.where(causal, s, NEG_INF)
                m_tile = jnp.max(s, axis=1, keepdims=True)
                m_new = m_tile if m is None else jnp.maximum(m, m_tile)
                p = jnp.exp2(s - m_new)
                l_tile = jnp.sum(p, axis=1, keepdims=True)
                pv = jnp.dot(p.astype(BF16), v_ref[0, keys, lanes], preferred_element_type=F32)
                if m is None:
                    l, acc = l_tile, pv
                else:
                    alpha = jnp.exp2(m - m_new)
                    l, acc = alpha * l + l_tile, alpha * acc + pv
                m = m_new
            o_ref[0, rows, lanes] = (acc / l).astype(o_ref.dtype)


def _fox_attention(qkv, dec, batch, seq, *, tq=512, heads_per_step=2):
    width = FOX_HEADS * FOX_HEAD_DIM
    tq = min(tq, seq)
    assert seq % tq == 0 and FOX_HEADS % heads_per_step == 0
    head_groups = FOX_HEADS // heads_per_step
    group_width = heads_per_step * FOX_HEAD_DIM
    qkv = qkv.reshape(batch, seq, 3 * width)
    dec_k = dec[:, :, :FOX_HEADS].transpose(0, 2, 1).reshape(batch, FOX_HEADS, seq // tq, tq)

    def group_spec(first_block):
        return pl.BlockSpec((1, seq, group_width), lambda b, h: (b, 0, first_block + h))

    return pl.pallas_call(
        functools.partial(_fox_kernel, tq=tq, n_tiles=seq // tq, heads_per_step=heads_per_step),
        grid=(batch, head_groups),
        in_specs=[
            group_spec(0), group_spec(head_groups), group_spec(2 * head_groups),
            pl.BlockSpec((1, seq, LANES), lambda b, h: (b, 0, 0)),
            pl.BlockSpec((1, heads_per_step, seq // tq, tq), lambda b, h: (b, h, 0, 0)),
        ],
        out_specs=group_spec(0),
        out_shape=jax.ShapeDtypeStruct((batch, seq, width), BF16),
        compiler_params=_compiler_params(
            ("parallel", "parallel"),
            pipelined=[((seq, group_width), BF16)] * 4 + [((seq, LANES), F32), ((FOX_HEADS, tq), F32)],
            resident=[((tq, tq), F32)] * 12),
        name="fox_attention",
    )(qkv, qkv, qkv, dec, dec_k).reshape(batch * seq, width)


def _gmlp_layer(h, w_in, v_gain, ws, bs, w_out):
    z = _norm_matmul(h, w_in, gelu=True)
    return _gmlp_mix(z, v_gain, ws, bs, w_out, h)


def _swa_layer(h, w_in, sinks, w_out, batch, seq):
    d = h.shape[1]
    qw = SWA_Q_HEADS * SWA_HEAD_DIM
    kvw = SWA_KV_HEADS * SWA_HEAD_DIM

    def duplicate_heads(w):
        w = w.reshape(d, SWA_KV_HEADS, 1, SWA_HEAD_DIM)
        return jnp.broadcast_to(w, (d, SWA_KV_HEADS, 2, SWA_HEAD_DIM)).reshape(d, SWA_KV_HEADS * LANES)

    w = jnp.concatenate([w_in[:, :qw], duplicate_heads(w_in[:, qw:qw + kvw]),
                         duplicate_heads(w_in[:, qw + kvw:])], axis=1)
    qkv = _norm_matmul(h, w)
    o = _swa_attention(qkv, sinks, batch, seq)
    return _matmul_residual(o, w_out, h)


def _fox_layer(h, w_in, b_f, w_out, batch, seq):
    d = h.shape[1]
    width = FOX_HEADS * FOX_HEAD_DIM
    w_f = jnp.zeros((d, LANES), BF16).at[:, :FOX_HEADS].set(w_in[:, 3 * width:])
    qkv, fl = _norm_matmul(h, w_in, w_f, n_cols=3 * width, scaled_cols=width, scale=FOX_Q_SCALE)
    dec = _fox_decay(fl, b_f, batch, seq)
    o = _fox_attention(qkv, dec, batch, seq)
    return _matmul_residual(o, w_out, h)


def kernel(x, l0_ffn1_norm, l0_ffn1_wi, l0_ffn1_wo, l0_mix_norm, l0_mix_win, l0_gmlp_vnorm, l0_gmlp_ws, l0_gmlp_bs, l0_mix_wout, l0_ffn2_norm, l0_ffn2_wi, l0_ffn2_wo, l1_ffn1_norm, l1_ffn1_wi, l1_ffn1_wo, l1_mix_norm, l1_mix_win, l1_swa_sinks, l1_mix_wout, l1_ffn2_norm, l1_ffn2_wi, l1_ffn2_wo, l2_ffn1_norm, l2_ffn1_wi, l2_ffn1_wo, l2_mix_norm, l2_mix_win, l2_fox_bf, l2_mix_wout, l2_ffn2_norm, l2_ffn2_wi, l2_ffn2_wo, l3_ffn1_norm, l3_ffn1_wi, l3_ffn1_wo, l3_mix_norm, l3_mix_win, l3_gmlp_vnorm, l3_gmlp_ws, l3_gmlp_bs, l3_mix_wout, l3_ffn2_norm, l3_ffn2_wi, l3_ffn2_wo, final_norm):
    batch, seq, d = x.shape
    h = x.reshape(batch * seq, d)

    ffn_weights = [(l0_ffn1_norm, l0_ffn1_wi, l0_ffn1_wo), (l0_ffn2_norm, l0_ffn2_wi, l0_ffn2_wo),
                   (l1_ffn1_norm, l1_ffn1_wi, l1_ffn1_wo), (l1_ffn2_norm, l1_ffn2_wi, l1_ffn2_wo),
                   (l2_ffn1_norm, l2_ffn1_wi, l2_ffn1_wo), (l2_ffn2_norm, l2_ffn2_wi, l2_ffn2_wo),
                   (l3_ffn1_norm, l3_ffn1_wi, l3_ffn1_wo), (l3_ffn2_norm, l3_ffn2_wi, l3_ffn2_wo)]
    gain0, wi0, wo0 = ffn_weights[0]
    cast = [((gain0[:, None] * wi0).astype(BF16), (0.5 * wo0).astype(BF16))]

    def ffn(h, final_gain=None, cast_extra=()):
        index = len(cast) - 1
        wi, wo_half = cast[index]
        next_weights = ffn_weights[index + 1] if index + 1 < len(ffn_weights) else None
        h, cast_next, extra = _ffn(h, wi, wo_half, final_gain, next_weights, cast_extra)
        cast.append(cast_next)
        return h, extra

    h, (w_in, w_out) = ffn(h, cast_extra=((l0_mix_win, l0_mix_norm), (l0_mix_wout, None)))
    h = _gmlp_layer(h, w_in, l0_gmlp_vnorm, l0_gmlp_ws, l0_gmlp_bs, w_out)
    h, _ = ffn(h)

    h, (w_in, w_out) = ffn(h, cast_extra=((l1_mix_win, l1_mix_norm), (l1_mix_wout, None)))
    h = _swa_layer(h, w_in, l1_swa_sinks, w_out, batch, seq)
    h, _ = ffn(h)

    h, (w_in, w_out) = ffn(h, cast_extra=((l2_mix_win, l2_mix_norm), (l2_mix_wout, None)))
    h = _fox_layer(h, w_in, l2_fox_bf, w_out, batch, seq)
    h, _ = ffn(h)

    h, (w_in, w_out) = ffn(h, cast_extra=((l3_mix_win, l3_mix_norm), (l3_mix_wout, None)))
    h = _gmlp_layer(h, w_in, l3_gmlp_vnorm, l3_gmlp_ws, l3_gmlp_bs, w_out)
    h, _ = ffn(h, final_norm)
    return h.reshape(batch, seq, d)
```
